```python
import math
import jax, jax.numpy as jnp
from jax import lax
import numpy as np


D_MODEL = 1024
BATCH = 4
SEQ = 4096
DEPTH = 2

HEAD_DIM = 64
N_HEADS_MIX = 8
BRANCH_WIDTH = N_HEADS_MIX * HEAD_DIM
N_BRANCHES = 3
IDX_HEADS = 16
IDX_DIM = 64
TOPK_MAX = 256
Q_BLOCK = 128
D_FF = 4 * D_MODEL
LN_EPS = 1e-5
DEEPNORM_ALPHA = (2.0 * DEPTH) ** 0.25
DEEPNORM_BETA = (8.0 * DEPTH) ** -0.25
FORGET_BIAS_MEAN = 3.0

IN_SIZES = (
    BRANCH_WIDTH, BRANCH_WIDTH, BRANCH_WIDTH,
    IDX_HEADS * IDX_DIM, IDX_DIM, IDX_HEADS,
    BRANCH_WIDTH, BRANCH_WIDTH, BRANCH_WIDTH,
    N_HEADS_MIX,
    BRANCH_WIDTH, BRANCH_WIDTH, BRANCH_WIDTH,
    N_BRANCHES * D_MODEL,
)
V_SECTIONS = (2, 8, 12)
IN_COLS = sum(IN_SIZES)

kernel_name = 'hybrid_dsa_fox_stickbreak_gated_deepnorm'


def _layer_norm(x, g, b):
    xf = x.astype(jnp.float32)
    mu = jnp.mean(xf, axis=-1, keepdims=True)
    xc = xf - mu
    var = jnp.mean(xc * xc, axis=-1, keepdims=True)
    y = xc * lax.rsqrt(var + LN_EPS) * g.astype(jnp.float32) + b.astype(jnp.float32)
    return y.astype(x.dtype)


def _alibi_slopes(n):
    return 2.0 ** (-8.0 * jnp.arange(1, n + 1, dtype=jnp.float32) / n)


def _to_blocks(a):
    b, s = a.shape[:2]
    a = a.reshape((b, s // Q_BLOCK, Q_BLOCK) + a.shape[2:])
    return jnp.moveaxis(a, 1, 0)


def _from_blocks(a):
    a = jnp.moveaxis(a, 0, 1)
    b, nb, qb = a.shape[:3]
    return a.reshape(b, nb * qb, -1)


def _dsa_attention(q, k, v, q_idx, k_idx, w_idx, slopes):
    s_len = q.shape[1]
    topk = min(TOPK_MAX, s_len // 4)
    pos = jnp.arange(s_len)
    scale = HEAD_DIM ** -0.5

    def block(args):
        qb, qib, wb, tq = args
        dots = jnp.einsum('bqhd,bkd->bqhk', qib, k_idx, preferred_element_type=jnp.float32)
        score = jnp.einsum('bqhk,bqh->bqk', jax.nn.relu(dots), wb.astype(jnp.float32))
        causal = pos[None, :] <= tq[:, None]
        score = jnp.where(causal[None], score, -jnp.inf)
        _, idx = lax.top_k(score, topk)
        valid = idx <= tq[None, :, None]
        k_sel = jax.vmap(lambda kb, ib: kb[ib])(k, idx)
        v_sel = jax.vmap(lambda vb, ib: vb[ib])(v, idx)
        logits = jnp.einsum('bqhd,bqkhd->bhqk', qb, k_sel, preferred_element_type=jnp.float32) * scale
        dist = (tq[None, :, None] - idx).astype(jnp.float32)
        logits = logits - slopes[None, :, None, None] * dist[:, None]
        logits = jnp.where(valid[:, None], logits, -jnp.inf)
        p = jax.nn.softmax(logits, axis=-1).astype(v.dtype)
        return jnp.einsum('bhqk,bqkhd->bqhd', p, v_sel)

    out = lax.map(block, (_to_blocks(q), _to_blocks(q_idx), _to_blocks(w_idx), pos.reshape(-1, Q_BLOCK)))
    return _from_blocks(out)


def _forgetting_attention(q, k, v, log_f):
    s_len = q.shape[1]
    pos = jnp.arange(s_len)
    scale = HEAD_DIM ** -0.5
    c = jnp.cumsum(log_f, axis=1)
    c_k = jnp.transpose(c, (0, 2, 1))[:, :, None, :]

    def block(args):
        qb, cq, tq = args
        logits = jnp.einsum('bqhd,bkhd->bhqk', qb, k, preferred_element_type=jnp.float32) * scale
        logits = logits + jnp.transpose(cq, (0, 2, 1))[..., None] - c_k
        causal = pos[None, :] <= tq[:, None]
        logits = jnp.where(causal[None, None], logits, -jnp.inf)
        p = jax.nn.softmax(logits, axis=-1).astype(v.dtype)
        return jnp.einsum('bhqk,bkhd->bqhd', p, v)

    out = lax.map(block, (_to_blocks(q), _to_blocks(c), pos.reshape(-1, Q_BLOCK)))
    return _from_blocks(out)


def _stick_breaking_attention(q, k, v):
    s_len = q.shape[1]
    pos = jnp.arange(s_len)
    scale = HEAD_DIM ** -0.5

    def block(args):
        qb, tq = args
        z = jnp.einsum('bqhd,bkhd->bhqk', qb, k, preferred_element_type=jnp.float32) * scale
        strict = (pos[None, :] < tq[:, None])[None, None]
        log_beta = jax.nn.log_sigmoid(z)
        log_one_minus = jnp.where(strict, jax.nn.log_sigmoid(-z), 0.0)
        later = lax.cumsum(log_one_minus, axis=3, reverse=True) - log_one_minus
        a = jnp.where(strict, jnp.exp(log_beta + later), 0.0).astype(v.dtype)
        return jnp.einsum('bhqk,bkhd->bqhd', a, v)

    out = lax.map(block, (_to_blocks(q), pos.reshape(-1, Q_BLOCK)))
    return _from_blocks(out)


def setup_inputs(seed: int = 0) -> dict:
    key = jax.random.key(seed)
    ks = jax.random.split(key, 12)
    x = jax.random.normal(ks[0], (BATCH, SEQ, D_MODEL), jnp.float32)
    sec_keys = jax.random.split(ks[1], len(IN_SIZES))
    secs = []
    for i, size in enumerate(IN_SIZES):
        sc = D_MODEL ** -0.5 * (DEEPNORM_BETA if i in V_SECTIONS else 1.0)
        secs.append(jax.random.normal(sec_keys[i], (DEPTH, D_MODEL, size), jnp.float32) * sc)
    w_in = jnp.concatenate(secs, axis=-1)
    b_forget = FORGET_BIAS_MEAN + 0.5 * jax.random.normal(ks[2], (DEPTH, N_HEADS_MIX), jnp.float32)
    w_branch = jax.random.normal(ks[3], (DEPTH, N_BRANCHES, BRANCH_WIDTH, D_MODEL), jnp.float32) * (BRANCH_WIDTH ** -0.5 * DEEPNORM_BETA)
    w_out = jax.random.normal(ks[4], (DEPTH, D_MODEL, D_MODEL), jnp.float32) * (D_MODEL ** -0.5 * DEEPNORM_BETA)
    ln1_g = 1.0 + 0.05 * jax.random.normal(ks[5], (DEPTH, D_MODEL), jnp.float32)
    ln1_b = 0.02 * jax.random.normal(ks[6], (DEPTH, D_MODEL), jnp.float32)
    w_ff1 = jax.random.normal(ks[7], (DEPTH, D_MODEL, D_FF), jnp.float32) * (D_MODEL ** -0.5 * DEEPNORM_BETA)
    w_ff2 = jax.random.normal(ks[8], (DEPTH, D_FF, D_MODEL), jnp.float32) * (D_FF ** -0.5 * DEEPNORM_BETA)
    ln2_g = 1.0 + 0.05 * jax.random.normal(ks[9], (DEPTH, D_MODEL), jnp.float32)
    ln2_b = 0.02 * jax.random.normal(ks[10], (DEPTH, D_MODEL), jnp.float32)
    return {'x': x, 'w_in': w_in, 'b_forget': b_forget, 'w_branch': w_branch, 'w_out': w_out,
            'ln1_g': ln1_g, 'ln1_b': ln1_b, 'w_ff1': w_ff1, 'w_ff2': w_ff2,
            'ln2_g': ln2_g, 'ln2_b': ln2_b}


def reference(x, w_in, b_forget, w_branch, w_out, ln1_g, ln1_b, w_ff1, w_ff2, ln2_g, ln2_b):
    b, s, _ = x.shape
    slopes = _alibi_slopes(N_HEADS_MIX)
    split_at = [int(v) for v in np.cumsum(IN_SIZES)[:-1]]

    def heads(a):
        return a.reshape(b, s, N_HEADS_MIX, HEAD_DIM)

    for layer in range(DEPTH):
        proj = jnp.einsum('bsd,dc->bsc', x, w_in[layer])
        (qa, ka, va, qi, ki, wi, qf, kf, vf, fg, qs, ks_, vs, gates) = jnp.split(proj, split_at, axis=-1)
        o_a = _dsa_attention(heads(qa), heads(ka), heads(va),
                             qi.reshape(b, s, IDX_HEADS, IDX_DIM), ki, wi, slopes)
        log_f = jax.nn.log_sigmoid((fg + b_forget[layer]).astype(jnp.float32))
        o_b = _forgetting_attention(heads(qf), heads(kf), heads(vf), log_f)
        o_c = _stick_breaking_attention(heads(qs), heads(ks_), heads(vs))
        branches = jnp.stack([o_a, o_b, o_c], axis=0)
        up = jnp.einsum('ibsw,iwd->bsid', branches, w_branch[layer])
        g = jax.nn.sigmoid(gates.reshape(b, s, N_BRANCHES, D_MODEL))
        merged = jnp.sum(up * g, axis=2)
        y = jnp.einsum('bsd,de->bse', merged, w_out[layer])
        x = _layer_norm(DEEPNORM_ALPHA * x + y, ln1_g[layer], ln1_b[layer])
        h = jnp.square(jax.nn.relu(jnp.einsum('bsd,df->bsf', x, w_ff1[layer])))
        y = jnp.einsum('bsf,fd->bsd', h, w_ff2[layer])
        x = _layer_norm(DEEPNORM_ALPHA * x + y, ln2_g[layer], ln2_b[layer])
    return x
```

```python
import functools

import numpy as np
import jax
import jax.numpy as jnp
from jax import lax
from jax.experimental import pallas as pl
from jax.experimental.pallas import tpu as pltpu

HEAD_DIM = 64
N_HEADS = 8
BRANCH = N_HEADS * HEAD_DIM
IDX_HEADS = 16
IDX_DIM = 64
TOPK_MAX = 256
N_BRANCHES = 3
LN_EPS = 1e-5

LANES = 128
CHUNK = 256
N_GROUPS = 12
G_DSA_Q, G_DSA_K, G_FOX_Q, G_FOX_K, G_SB_Q, G_SB_K, G_V_AB, G_V_C_KI, G_QI, G_GATES = 0, 1, 2, 3, 4, 5, 6, 7, 8, 9
AUG_LANE = HEAD_DIM
XE_A, XE_B, XE_ONE, XE_C = 0, 1, 2, 3
NEG_BIG = -1e30
INT_MIN = -(2 ** 31)

_NT = (((1,), (1,)), ((), ()))


def _log_sigmoid(x):
    return jnp.minimum(x, 0.0) - jnp.log1p(jnp.exp(-jnp.abs(x)))


def _split3(x):
    a = x.astype(jnp.bfloat16)
    r = x - a.astype(jnp.float32)
    b = r.astype(jnp.bfloat16)
    c = (r - b.astype(jnp.float32)).astype(jnp.bfloat16)
    return a, b, c


def _layer_norm(z, g, b):
    mu = jnp.mean(z, axis=-1, keepdims=True)
    zc = z - mu
    var = jnp.mean(zc * zc, axis=-1, keepdims=True)
    return zc * lax.rsqrt(var + LN_EPS) * g + b


def _params(sem, vmem_mb):
    return pltpu.CompilerParams(dimension_semantics=sem, vmem_limit_bytes=vmem_mb * 1024 * 1024)


def _prep_kernel(x_ref, ws_ref, bs_ref, xb_ref, xe_ref, sm_ref, carry_ref):
    s_blk = pl.program_id(1)
    t = x_ref.shape[1]
    xb = x_ref[0].astype(jnp.bfloat16)
    xb_ref[0] = xb
    sm = jnp.dot(xb, ws_ref[...], preferred_element_type=jnp.float32)
    sm_ref[0] = sm
    lf = _log_sigmoid(sm + bs_ref[...])

    @pl.when(s_blk == 0)
    def _():
        carry_ref[...] = jnp.zeros_like(carry_ref)

    row = lax.broadcasted_iota(jnp.int32, (t, t), 0)
    col = lax.broadcasted_iota(jnp.int32, (t, t), 1)
    tri = jnp.where(col <= row, 1.0, 0.0).astype(jnp.bfloat16)
    cs = carry_ref[0:1, :]
    for part in _split3(lf):
        cs = cs + jnp.dot(tri, part, preferred_element_type=jnp.float32)
    carry_ref[...] = jnp.broadcast_to(cs[t - 1:t, :], carry_ref.shape)

    er = lax.broadcasted_iota(jnp.int32, (LANES, LANES), 0)
    ec = lax.broadcasted_iota(jnp.int32, (LANES, LANES), 1)
    xe = jnp.zeros((t, LANES), jnp.float32)
    for p, part in enumerate(_split3(cs)):
        place = jnp.where(ec == er + (XE_C + N_HEADS * p), jnp.where(er < N_HEADS, 1.0, 0.0), 0.0)
        xe = xe + jnp.dot(part, place.astype(jnp.bfloat16), preferred_element_type=jnp.float32)
    pos = s_blk * t + lax.broadcasted_iota(jnp.int32, (t, LANES), 0)
    lane = lax.broadcasted_iota(jnp.int32, (t, LANES), 1)
    xe = jnp.where(lane == XE_A, jnp.right_shift(pos, 8).astype(jnp.float32), xe)
    xe = jnp.where(lane == XE_B, (pos & (CHUNK - 1)).astype(jnp.float32), xe)
    xe = jnp.where(lane == XE_ONE, 1.0, xe)
    xe_ref[0] = xe.astype(jnp.bfloat16)


def _prep(x, w_small, b_small, t=512):
    b, s, d = x.shape
    return pl.pallas_call(
        _prep_kernel,
        grid=(b, s // t),
        in_specs=[pl.BlockSpec((1, t, d), lambda bi, si: (bi, si, 0)),
                  pl.BlockSpec((d, LANES), lambda bi, si: (0, 0)),
                  pl.BlockSpec((1, LANES), lambda bi, si: (0, 0))],
        out_specs=[pl.BlockSpec((1, t, d), lambda bi, si: (bi, si, 0)),
                   pl.BlockSpec((1, t, LANES), lambda bi, si: (bi, si, 0)),
                   pl.BlockSpec((1, t, LANES), lambda bi, si: (bi, si, 0))],
        out_shape=[jax.ShapeDtypeStruct((b, s, d), jnp.bfloat16),
                   jax.ShapeDtypeStruct((b, s, LANES), jnp.bfloat16),
                   jax.ShapeDtypeStruct((b, s, LANES), jnp.float32)],
        scratch_shapes=[pltpu.VMEM((8, LANES), jnp.float32)],
        compiler_params=_params(("arbitrary", "arbitrary"), 32),
        name="prep",
    )(x, w_small, b_small)


def _proj_kernel(xb_ref, xe_ref, w_ref, wa_ref, o_ref):
    j = pl.program_id(0)
    acc = jnp.dot(xb_ref[0], w_ref[0], preferred_element_type=jnp.float32)
    acc = acc + jnp.dot(xe_ref[0], wa_ref[0], preferred_element_type=jnp.float32)

    def store(val):
        for c in range(N_HEADS):
            o_ref[0, 0, c] = val[:, c * LANES:(c + 1) * LANES].astype(jnp.bfloat16)

    @pl.when(j < G_GATES)
    def _():
        store(acc)

    @pl.when(j >= G_GATES)
    def _():
        store(jax.nn.sigmoid(acc))


def _proj(xb, xe, w_main, w_aug, t=1024):
    b, s, d = xb.shape
    ns = s // t
    return pl.pallas_call(
        _proj_kernel,
        grid=(N_GROUPS, b * ns),
        in_specs=[pl.BlockSpec((1, t, d), lambda j, i: (i // ns, i % ns, 0)),
                  pl.BlockSpec((1, t, LANES), lambda j, i: (i // ns, i % ns, 0)),
                  pl.BlockSpec((1, d, N_HEADS * LANES), lambda j, i: (j, 0, 0)),
                  pl.BlockSpec((1, LANES, N_HEADS * LANES), lambda j, i: (j, 0, 0))],
        out_specs=pl.BlockSpec((1, 1, N_HEADS, t, LANES), lambda j, i: (j, i // ns, 0, i % ns, 0)),
        out_shape=jax.ShapeDtypeStruct((N_GROUPS, b, N_HEADS, s, LANES), jnp.bfloat16),
        compiler_params=_params(("arbitrary", "arbitrary"), 48),
        name="proj",
    )(xb, xe, w_main, w_aug)


def _softmax_chunk(carry, s, vt):
    m, l, acc = carry
    m_new = jnp.maximum(m, jnp.max(s, axis=0, keepdims=True))
    alpha = jnp.exp(m - m_new)
    p = jnp.exp(s - m_new)
    l = alpha * l + jnp.sum(p, axis=0, keepdims=True)
    acc = alpha * acc + jnp.dot(vt, p.astype(jnp.bfloat16), preferred_element_type=jnp.float32)
    return m_new, l, acc


def _softmax_init(tq):
    return (jnp.full((1, tq), NEG_BIG, jnp.float32), jnp.zeros((1, tq), jnp.float32),
            jnp.zeros((HEAD_DIM, tq), jnp.float32))


def _dsa_kernel(q_ref, k_ref, vt_ref, qi_ref, ki_ref, wt_ref, o_ref, keys_ref, qz_ref, *, topk):
    i = pl.program_id(1)
    tq = q_ref.shape[4]
    lane = lax.broadcasted_iota(jnp.int32, (tq, LANES), 1)
    for p in range(IDX_HEADS // 2):
        qp = qi_ref[0, 0, p, 0]
        qz_ref[2 * p] = jnp.where(lane < IDX_DIM, qp, jnp.zeros_like(qp))
        qz_ref[2 * p + 1] = jnp.where(lane >= IDX_DIM, qp, jnp.zeros_like(qp))
    rel = (lax.broadcasted_iota(jnp.int32, (CHUNK, tq), 0)
           - lax.broadcasted_iota(jnp.int32, (CHUNK, tq), 1))

    def score_chunk(j, _):
        kc = ki_ref[0, 0, 0, j]
        sc = jnp.zeros((CHUNK, tq), jnp.float32)
        for h in range(IDX_HEADS):
            d = lax.dot_general(kc, qz_ref[h], _NT, preferred_element_type=jnp.float32)
            sc = sc + jnp.maximum(d, 0.0) * wt_ref[0, h:h + 1, :]
        bits = pltpu.bitcast(sc, jnp.int32)
        key = bits ^ ((bits >> 31) & jnp.int32(0x7FFFFFFF))
        keys_ref[j] = jnp.where(rel <= (i - j) * CHUNK, key, jnp.int32(INT_MIN))
        return 0

    lax.fori_loop(0, i + 1, score_chunk, 0)

    def count_ge(cand):
        def body(j, acc):
            ind = jnp.where(keys_ref[j] >= cand, 1.0, 0.0)
            return acc + jnp.sum(ind.reshape(CHUNK // 8, 8, tq), axis=0)
        acc = lax.fori_loop(0, i + 1, body, jnp.zeros((8, tq), jnp.float32))
        return jnp.sum(acc, axis=0, keepdims=True)

    thr = jnp.where(count_ge(jnp.zeros((1, tq), jnp.int32)) >= topk, jnp.int32(0), jnp.int32(INT_MIN))

    def bit_body(n, thr):
        cand = thr + jnp.left_shift(jnp.int32(1), 30 - n)
        return jnp.where(count_ge(cand) >= topk, cand, thr)

    thr = lax.fori_loop(0, 31, bit_body, thr)
    thr = jnp.maximum(thr, jnp.int32(INT_MIN + 1))

    def head(h, _):
        qh = q_ref[0, 0, h, 0]

        def chunk(j, carry):
            s = lax.dot_general(k_ref[0, 0, h, j], qh, _NT, preferred_element_type=jnp.float32)
            s = jnp.where(keys_ref[j] >= thr, s, NEG_BIG)
            return _softmax_chunk(carry, s, vt_ref[0, h, j])

        m, l, acc = lax.fori_loop(0, i + 1, chunk, _softmax_init(tq))
        o_ref[0, h] = (acc / l).astype(o_ref.dtype)
        return 0

    lax.fori_loop(0, N_HEADS, head, 0)


def _dsa(pq, vt, wt, topk):
    _, b, _, nc, _, _ = pq.shape
    tq = CHUNK
    kern = functools.partial(_dsa_kernel, topk=float(topk))
    return pl.pallas_call(
        kern,
        grid=(b, nc),
        in_specs=[pl.BlockSpec((1, 1, N_HEADS, 1, tq, LANES), lambda bi, i: (G_DSA_Q, bi, 0, i, 0, 0)),
                  pl.BlockSpec((1, 1, N_HEADS, nc, CHUNK, LANES), lambda bi, i: (G_DSA_K, bi, 0, 0, 0, 0)),
                  pl.BlockSpec((1, N_HEADS, nc, HEAD_DIM, CHUNK), lambda bi, i: (bi, 0, 0, 0, 0)),
                  pl.BlockSpec((1, 1, N_HEADS, 1, tq, LANES), lambda bi, i: (G_QI, bi, 0, i, 0, 0)),
                  pl.BlockSpec((1, 1, 1, nc, CHUNK, LANES), lambda bi, i: (G_V_C_KI, bi, 4, 0, 0, 0)),
                  pl.BlockSpec((1, IDX_HEADS, tq), lambda bi, i: (bi, 0, i))],
        out_specs=pl.BlockSpec((1, N_HEADS, HEAD_DIM, tq), lambda bi, i: (bi, 0, 0, i)),
        out_shape=jax.ShapeDtypeStruct((b, N_HEADS, HEAD_DIM, nc * CHUNK), jnp.bfloat16),
        scratch_shapes=[pltpu.VMEM((nc, CHUNK, tq), jnp.int32),
                        pltpu.VMEM((IDX_HEADS, tq, LANES), jnp.bfloat16)],
        compiler_params=_params(("arbitrary", "arbitrary"), 56),
        name="dsa",
    )(pq, pq, vt, pq, pq, wt)


def _fox_kernel(q_ref, k_ref, vt_ref, o_ref):
    i = pl.program_id(1)
    tq = q_ref.shape[4]
    causal = (lax.broadcasted_iota(jnp.int32, (CHUNK, tq), 0)
              <= lax.broadcasted_iota(jnp.int32, (CHUNK, tq), 1))

    def head(h, _):
        qh = q_ref[0, 0, h, 0]

        def chunk(j, carry, masked):
            s = lax.dot_general(k_ref[0, 0, h, j], qh, _NT, preferred_element_type=jnp.float32)
            if masked:
                s = jnp.where(causal, s, NEG_BIG)
            return _softmax_chunk(carry, s, vt_ref[0, h, j])

        carry = lax.fori_loop(0, i, lambda j, c: chunk(j, c, False), _softmax_init(tq))
        m, l, acc = chunk(i, carry, True)
        o_ref[0, h] = (acc / l).astype(o_ref.dtype)
        return 0

    lax.fori_loop(0, N_HEADS, head, 0)


def _sb_kernel(q_ref, k_ref, vt_ref, o_ref):
    i = pl.program_id(1)
    tq = q_ref.shape[4]
    strict = (lax.broadcasted_iota(jnp.int32, (CHUNK, tq), 0)
              < lax.broadcasted_iota(jnp.int32, (CHUNK, tq), 1))
    upper = jnp.where(lax.broadcasted_iota(jnp.int32, (CHUNK, CHUNK), 1)
                      > lax.broadcasted_iota(jnp.int32, (CHUNK, CHUNK), 0), 1.0, 0.0).astype(jnp.bfloat16)

    def head(h, _):
        qh = q_ref[0, 0, h, 0]

        def chunk(j, carry, diag):
            later_blocks, acc = carry
            z = lax.dot_general(k_ref[0, 0, h, j], qh, _NT, preferred_element_type=jnp.float32)
            log_beta = _log_sigmoid(z)
            lom = log_beta - z
            if diag:
                lom = jnp.where(strict, lom, 0.0)
            hi = lom.astype(jnp.bfloat16)
            lo = (lom - hi.astype(jnp.float32)).astype(jnp.bfloat16)
            within = (jnp.dot(upper, hi, preferred_element_type=jnp.float32)
                      + jnp.dot(upper, lo, preferred_element_type=jnp.float32))
            a = jnp.exp(log_beta + within + later_blocks)
            if diag:
                a = jnp.where(strict, a, 0.0)
            acc = acc + jnp.dot(vt_ref[0, h, j], a.astype(jnp.bfloat16), preferred_element_type=jnp.float32)
            later_blocks = later_blocks + within[0:1, :] + lom[0:1, :]
            return later_blocks, acc

        carry = chunk(i, (jnp.zeros((1, tq), jnp.float32), jnp.zeros((HEAD_DIM, tq), jnp.float32)), True)
        _, acc = lax.fori_loop(0, i, lambda n, c: chunk(i - 1 - n, c, False), carry)
        o_ref[0, h] = acc.astype(o_ref.dtype)
        return 0

    lax.fori_loop(0, N_HEADS, head, 0)


def _causal_attention(kernel, name, pq, vt, g_q, g_k):
    _, b, _, nc, _, _ = pq.shape
    tq = CHUNK
    return pl.pallas_call(
        kernel,
        grid=(b, nc),
        in_specs=[pl.BlockSpec((1, 1, N_HEADS, 1, tq, LANES), lambda bi, i: (g_q, bi, 0, i, 0, 0)),
                  pl.BlockSpec((1, 1, N_HEADS, nc, CHUNK, LANES), lambda bi, i: (g_k, bi, 0, 0, 0, 0)),
                  pl.BlockSpec((1, N_HEADS, nc, HEAD_DIM, CHUNK), lambda bi, i: (bi, 0, 0, 0, 0))],
        out_specs=pl.BlockSpec((1, N_HEADS, HEAD_DIM, tq), lambda bi, i: (bi, 0, 0, i)),
        out_shape=jax.ShapeDtypeStruct((b, N_HEADS, HEAD_DIM, nc * CHUNK), jnp.bfloat16),
        compiler_params=_params(("arbitrary", "arbitrary"), 48),
        name=name,
    )(pq, pq, vt)


def _merge_kernel(oa_ref, ob_ref, oc_ref, g_ref, x_ref, wb_ref, wo_ref, gam_ref, bet_ref, o_ref, *, alpha):
    ups = [jnp.dot(o[0], wb_ref[n], preferred_element_type=jnp.float32)
           for n, o in enumerate((oa_ref, ob_ref, oc_ref))]
    cols = []
    for c in range(N_HEADS):
        sl = slice(c * LANES, (c + 1) * LANES)
        mc = ups[0][:, sl] * g_ref[0, 0, c].astype(jnp.float32)
        for n in range(1, N_BRANCHES):
            mc = mc + ups[n][:, sl] * g_ref[n, 0, c].astype(jnp.float32)
        cols.append(mc.astype(jnp.bfloat16))
    merged = jnp.concatenate(cols, axis=1)
    y = jnp.dot(merged, wo_ref[...], preferred_element_type=jnp.float32)
    o_ref[0] = _layer_norm(alpha * x_ref[0] + y, gam_ref[...], bet_ref[...])


def _merge(oa, ob, oc, pq5, x, wb, wo, gam, bet, alpha, t=512):
    b, s, d = x.shape
    o_spec = pl.BlockSpec((1, t, BRANCH), lambda bi, si: (bi, si, 0))
    return pl.pallas_call(
        functools.partial(_merge_kernel, alpha=alpha),
        grid=(b, s // t),
        in_specs=[o_spec, o_spec, o_spec,
                  pl.BlockSpec((N_BRANCHES, 1, N_HEADS, t, LANES), lambda bi, si: (G_GATES // N_BRANCHES, bi, 0, si, 0)),
                  pl.BlockSpec((1, t, d), lambda bi, si: (bi, si, 0)),
                  pl.BlockSpec((N_BRANCHES, BRANCH, d), lambda bi, si: (0, 0, 0)),
                  pl.BlockSpec((d, d), lambda bi, si: (0, 0)),
                  pl.BlockSpec((1, d), lambda bi, si: (0, 0)),
                  pl.BlockSpec((1, d), lambda bi, si: (0, 0))],
        out_specs=pl.BlockSpec((1, t, d), lambda bi, si: (bi, si, 0)),
        out_shape=jax.ShapeDtypeStruct((b, s, d), jnp.float32),
        compiler_params=_params(("arbitrary", "arbitrary"), 48),
        name="merge",
    )(oa, ob, oc, pq5, x, wb, wo, gam, bet)


def _ffn_kernel(x_ref, w1_ref, w2_ref, gam_ref, bet_ref, o_ref, *, alpha, fchunk):
    x = x_ref[0]
    xb = x.astype(jnp.bfloat16)
    y = jnp.zeros_like(x)
    for c in range(w1_ref.shape[1] // fchunk):
        sl = slice(c * fchunk, (c + 1) * fchunk)
        h = jnp.dot(xb, w1_ref[:, sl], preferred_element_type=jnp.float32)
        h = jnp.square(jnp.maximum(h, 0.0)).astype(jnp.bfloat16)
        y = y + jnp.dot(h, w2_ref[sl, :], preferred_element_type=jnp.float32)
    o_ref[0] = _layer_norm(alpha * x + y, gam_ref[...], bet_ref[...])


def _ffn(x, w1, w2, gam, bet, alpha, t=512):
    b, s, d = x.shape
    f = w1.shape[1]
    return pl.pallas_call(
        functools.partial(_ffn_kernel, alpha=alpha, fchunk=1024),
        grid=(b, s // t),
        in_specs=[pl.BlockSpec((1, t, d), lambda bi, si: (bi, si, 0)),
                  pl.BlockSpec((d, f), lambda bi, si: (0, 0)),
                  pl.BlockSpec((f, d), lambda bi, si: (0, 0)),
                  pl.BlockSpec((1, d), lambda bi, si: (0, 0)),
                  pl.BlockSpec((1, d), lambda bi, si: (0, 0))],
        out_specs=pl.BlockSpec((1, t, d), lambda bi, si: (bi, si, 0)),
        out_shape=jax.ShapeDtypeStruct((b, s, d), jnp.float32),
        compiler_params=_params(("arbitrary", "arbitrary"), 56),
        name="ffn",
    )(x, w1, w2, gam, bet)


def _pad_heads(w, n_heads, scale=1.0):
    d = w.shape[0]
    w = (w * scale).reshape(d, n_heads, HEAD_DIM)
    return jnp.pad(w, ((0, 0), (0, 0), (0, LANES - HEAD_DIM))).reshape(d, n_heads * LANES)


def _aug_weights(slopes):
    wa = np.zeros((N_GROUPS, LANES, N_HEADS * LANES), np.float32)
    for h in range(N_HEADS):
        base = h * LANES + AUG_LANE
        sl = float(slopes[h])
        wa[G_DSA_Q, XE_A, base + 0] = -sl * CHUNK
        wa[G_DSA_Q, XE_B, base + 1] = -sl
        wa[G_DSA_Q, XE_ONE, base + 2] = 1.0
        wa[G_DSA_Q, XE_ONE, base + 3] = 1.0
        wa[G_DSA_K, XE_ONE, base + 0] = 1.0
        wa[G_DSA_K, XE_ONE, base + 1] = 1.0
        wa[G_DSA_K, XE_A, base + 2] = sl * CHUNK
        wa[G_DSA_K, XE_B, base + 3] = sl
        for p in range(3):
            wa[G_FOX_Q, XE_ONE, base + p] = 1.0
            wa[G_FOX_K, XE_C + N_HEADS * p + h, base + p] = -1.0
    return wa


def _layer_weights(w_in, b_forget, slopes):
    d = w_in.shape[0]
    sizes = (BRANCH, BRANCH, BRANCH, IDX_HEADS * IDX_DIM, IDX_DIM, IDX_HEADS,
             BRANCH, BRANCH, BRANCH, N_HEADS, BRANCH, BRANCH, BRANCH, N_BRANCHES * d)
    offs = np.concatenate([[0], np.cumsum(sizes)])
    (qa, ka, va, qi, ki, wi, qf, kf, vf, fg, qs, ks, vs, gates) = [w_in[:, offs[n]:offs[n + 1]] for n in range(len(sizes))]
    scale = HEAD_DIM ** -0.5
    zeros = jnp.zeros((d, N_HEADS * LANES - BRANCH - 2 * IDX_DIM), w_in.dtype)
    groups = [_pad_heads(qa, N_HEADS, scale), _pad_heads(ka, N_HEADS),
              _pad_heads(qf, N_HEADS, scale), _pad_heads(kf, N_HEADS),
              _pad_heads(qs, N_HEADS, scale), _pad_heads(ks, N_HEADS),
              jnp.concatenate([va, vf], axis=1),
              jnp.concatenate([vs, ki, ki, zeros], axis=1),
              qi] + [gates[:, n * d:(n + 1) * d] for n in range(N_BRANCHES)]
    w_main = jnp.stack(groups).astype(jnp.bfloat16)
    w_small = jnp.concatenate([fg, wi, jnp.zeros((d, LANES - N_HEADS - IDX_HEADS), w_in.dtype)], axis=1)
    b_small = jnp.pad(b_forget, (0, LANES - N_HEADS)).reshape(1, LANES)
    return w_main, w_small.astype(jnp.bfloat16), b_small


def _v_transposed(vg, lo):
    b, _, s, _ = vg.shape
    v = vg[:, lo:lo + 4].reshape(b, 4, s // CHUNK, CHUNK, 2, HEAD_DIM)
    return v.transpose(0, 1, 4, 2, 5, 3).reshape(b, N_HEADS, s // CHUNK, HEAD_DIM, CHUNK)


def _untranspose(ot):
    b, h, dh, s = ot.shape
    return ot.transpose(0, 3, 1, 2).reshape(b, s, h * dh)


def kernel(x, w_in, b_forget, w_branch, w_out, ln1_g, ln1_b, w_ff1, w_ff2, ln2_g, ln2_b):
    depth = w_in.shape[0]
    b, s, d = x.shape
    assert s % CHUNK == 0 and d == N_HEADS * LANES
    alpha = (2.0 * depth) ** 0.25
    topk = min(TOPK_MAX, s // 4)
    slopes = 2.0 ** (-8.0 * np.arange(1, N_HEADS + 1, dtype=np.float64) / N_HEADS)
    w_aug_np = _aug_weights(slopes)
    assert np.array_equal(w_aug_np, w_aug_np.astype(jnp.bfloat16).astype(np.float32))
    w_aug = jnp.asarray(w_aug_np, jnp.bfloat16)
    nc = s // CHUNK

    for layer in range(depth):
        w_main, w_small, b_small = _layer_weights(w_in[layer], b_forget[layer], slopes)
        xb, xe, small = _prep(x, w_small, b_small)
        pq5 = _proj(xb, xe, w_main, w_aug)
        pq = pq5.reshape(N_GROUPS, b, N_HEADS, nc, CHUNK, LANES)
        wt = small[:, :, N_HEADS:N_HEADS + IDX_HEADS].transpose(0, 2, 1)
        vt_a = _v_transposed(pq5[G_V_AB], 0)
        vt_b = _v_transposed(pq5[G_V_AB], 4)
        vt_c = _v_transposed(pq5[G_V_C_KI], 0)
        o_a = _untranspose(_dsa(pq, vt_a, wt, topk))
        o_b = _untranspose(_causal_attention(_fox_kernel, "fox", pq, vt_b, G_FOX_Q, G_FOX_K))
        o_c = _untranspose(_causal_attention(_sb_kernel, "sb", pq, vt_c, G_SB_Q, G_SB_K))
        x = _merge(o_a, o_b, o_c, pq5, x, w_branch[layer].astype(jnp.bfloat16), w_out[layer].astype(jnp.bfloat16),
                   ln1_g[layer].reshape(1, d), ln1_b[layer].reshape(1, d), alpha)
        x = _ffn(x, w_ff1[layer].astype(jnp.bfloat16), w_ff2[layer].astype(jnp.bfloat16),
                 ln2_g[layer].reshape(1, d), ln2_b[layer].reshape(1, d), alpha)
    return x
```

```python
import functools

import numpy as np
import jax
import jax.numpy as jnp
from jax import lax
from jax.experimental import pallas as pl
from jax.experimental.pallas import tpu as pltpu

HEAD_DIM = 64
N_HEADS = 8
BRANCH = N_HEADS * HEAD_DIM
IDX_HEADS = 16
IDX_DIM = 64
TOPK_MAX = 256
N_BRANCHES = 3
LN_EPS = 1e-5

LANES = 128
CHUNK = 256
N_GROUPS = 12
G_DSA_Q, G_DSA_K, G_FOX_Q, G_FOX_K, G_SB_Q, G_SB_K, G_V_AB, G_V_C_KI, G_QI, G_GATES = 0, 1, 2, 3, 4, 5, 6, 7, 8, 9
AUG_LANE = HEAD_DIM
XE_A, XE_B, XE_ONE, XE_C = 0, 1, 2, 3
NEG_BIG = -1e30
EXP_IS_ZERO_BELOW = -105.0
INT_MIN = -(2 ** 31)

_NT = (((1,), (1,)), ((), ()))


def _log_sigmoid(x):
    return jnp.minimum(x, 0.0) - jnp.log(1.0 + jnp.exp(-jnp.abs(x)))


def _split3(x):
    a = x.astype(jnp.bfloat16)
    r = x - a.astype(jnp.float32)
    b = r.astype(jnp.bfloat16)
    c = (r - b.astype(jnp.float32)).astype(jnp.bfloat16)
    return a, b, c


def _layer_norm(z, g, b):
    mu = jnp.mean(z, axis=-1, keepdims=True)
    zc = z - mu
    var = jnp.mean(zc * zc, axis=-1, keepdims=True)
    return zc * lax.rsqrt(var + LN_EPS) * g + b


def _params(sem, vmem_mb):
    return pltpu.CompilerParams(dimension_semantics=sem, vmem_limit_bytes=vmem_mb * 1024 * 1024)


def _prep_kernel(x_ref, ws_ref, bs_ref, xb_ref, xe_ref, sm_ref, carry_ref):
    s_blk = pl.program_id(1)
    t = x_ref.shape[1]
    xb = x_ref[0].astype(jnp.bfloat16)
    xb_ref[0] = xb
    sm = jnp.dot(xb, ws_ref[...], preferred_element_type=jnp.float32)
    sm_ref[0] = sm
    fg = sm + bs_ref[...]
    lf = jnp.minimum(fg, 0.0) - jnp.log1p(jnp.exp(-jnp.abs(fg)))

    @pl.when(s_blk == 0)
    def _():
        carry_ref[...] = jnp.zeros_like(carry_ref)

    row = lax.broadcasted_iota(jnp.int32, (t, t), 0)
    col = lax.broadcasted_iota(jnp.int32, (t, t), 1)
    tri = jnp.where(col <= row, 1.0, 0.0).astype(jnp.bfloat16)
    cs = carry_ref[0:1, :]
    for part in _split3(lf):
        cs = cs + jnp.dot(tri, part, preferred_element_type=jnp.float32)
    carry_ref[...] = jnp.broadcast_to(cs[t - 1:t, :], carry_ref.shape)

    er = lax.broadcasted_iota(jnp.int32, (LANES, LANES), 0)
    ec = lax.broadcasted_iota(jnp.int32, (LANES, LANES), 1)
    xe = jnp.zeros((t, LANES), jnp.float32)
    for p, part in enumerate(_split3(cs)):
        place = jnp.where(ec == er + (XE_C + N_HEADS * p), jnp.where(er < N_HEADS, 1.0, 0.0), 0.0)
        xe = xe + jnp.dot(part, place.astype(jnp.bfloat16), preferred_element_type=jnp.float32)
    pos = s_blk * t + lax.broadcasted_iota(jnp.int32, (t, LANES), 0)
    lane = lax.broadcasted_iota(jnp.int32, (t, LANES), 1)
    xe = jnp.where(lane == XE_A, jnp.right_shift(pos, 8).astype(jnp.float32), xe)
    xe = jnp.where(lane == XE_B, (pos & (CHUNK - 1)).astype(jnp.float32), xe)
    xe = jnp.where(lane == XE_ONE, 1.0, xe)
    xe_ref[0] = xe.astype(jnp.bfloat16)


def _prep(x, w_small, b_small, t=512):
    b, s, d = x.shape
    return pl.pallas_call(
        _prep_kernel,
        grid=(b, s // t),
        in_specs=[pl.BlockSpec((1, t, d), lambda bi, si: (bi, si, 0)),
                  pl.BlockSpec((d, LANES), lambda bi, si: (0, 0)),
                  pl.BlockSpec((1, LANES), lambda bi, si: (0, 0))],
        out_specs=[pl.BlockSpec((1, t, d), lambda bi, si: (bi, si, 0)),
                   pl.BlockSpec((1, t, LANES), lambda bi, si: (bi, si, 0)),
                   pl.BlockSpec((1, t, LANES), lambda bi, si: (bi, si, 0))],
        out_shape=[jax.ShapeDtypeStruct((b, s, d), jnp.bfloat16),
                   jax.ShapeDtypeStruct((b, s, LANES), jnp.bfloat16),
                   jax.ShapeDtypeStruct((b, s, LANES), jnp.float32)],
        scratch_shapes=[pltpu.VMEM((8, LANES), jnp.float32)],
        compiler_params=_params(("arbitrary", "arbitrary"), 32),
        name="prep",
    )(x, w_small, b_small)


def _proj_kernel(xb_ref, xe_ref, w_ref, wa_ref, o_ref):
    j = pl.program_id(0)
    acc = jnp.dot(xb_ref[0], w_ref[0], preferred_element_type=jnp.float32)
    acc = acc + jnp.dot(xe_ref[0], wa_ref[0], preferred_element_type=jnp.float32)

    def store(val):
        for c in range(N_HEADS):
            o_ref[0, 0, c] = val[:, c * LANES:(c + 1) * LANES].astype(jnp.bfloat16)

    @pl.when(j < G_GATES)
    def _():
        store(acc)

    @pl.when(j >= G_GATES)
    def _():
        store(jax.nn.sigmoid(acc))


def _proj(xb, xe, w_main, w_aug, t=1024):
    b, s, d = xb.shape
    ns = s // t
    return pl.pallas_call(
        _proj_kernel,
        grid=(N_GROUPS, b * ns),
        in_specs=[pl.BlockSpec((1, t, d), lambda j, i: (i // ns, i % ns, 0)),
                  pl.BlockSpec((1, t, LANES), lambda j, i: (i // ns, i % ns, 0)),
                  pl.BlockSpec((1, d, N_HEADS * LANES), lambda j, i: (j, 0, 0)),
                  pl.BlockSpec((1, LANES, N_HEADS * LANES), lambda j, i: (j, 0, 0))],
        out_specs=pl.BlockSpec((1, 1, N_HEADS, t, LANES), lambda j, i: (j, i // ns, 0, i % ns, 0)),
        out_shape=jax.ShapeDtypeStruct((N_GROUPS, b, N_HEADS, s, LANES), jnp.bfloat16),
        compiler_params=_params(("arbitrary", "arbitrary"), 48),
        name="proj",
    )(xb, xe, w_main, w_aug)


STAGE_LAG = 8


def _staged(stages):
    for step in range(N_HEADS + STAGE_LAG * (len(stages) - 1)):
        for n, stage in enumerate(stages):
            if 0 <= step - STAGE_LAG * n < N_HEADS:
                stage(step - STAGE_LAG * n)


def _score_stage(h, s, s_ref, mloc_ref):
    s_ref[h] = s
    mloc_ref[h] = jnp.max(s, axis=0, keepdims=True)


def _softmax_stage(h, s_ref, mloc_ref, m_ref, l_ref, acc_ref, vt):
    m = m_ref[h]
    m_new = jnp.maximum(m, mloc_ref[h])
    alpha = jnp.exp(m - m_new)
    p = jnp.exp(s_ref[h] - m_new)
    m_ref[h] = m_new
    l_ref[h] = alpha * l_ref[h] + jnp.sum(p, axis=0, keepdims=True)
    acc_ref[h] = alpha * acc_ref[h] + jnp.dot(vt, p.astype(jnp.bfloat16), preferred_element_type=jnp.float32)


def _softmax_init(m_ref, l_ref, acc_ref):
    m_ref[...] = jnp.full(m_ref.shape, NEG_BIG, jnp.float32)
    l_ref[...] = jnp.zeros(l_ref.shape, jnp.float32)
    acc_ref[...] = jnp.zeros(acc_ref.shape, jnp.float32)


def _softmax_scratch(tq):
    vec = pltpu.VMEM((N_HEADS, 1, tq), jnp.float32)
    return [pltpu.VMEM((N_HEADS, CHUNK, tq), jnp.float32), vec, vec, vec,
            pltpu.VMEM((N_HEADS, HEAD_DIM, tq), jnp.float32)]


def _dsa_kernel(q_ref, k_ref, vt_ref, qi_ref, ki_ref, wt_ref, o_ref, keys_ref, qz_ref,
                s_ref, mloc_ref, m_ref, l_ref, acc_ref, *, topk):
    i = pl.program_id(1)
    tq = q_ref.shape[4]
    lane = lax.broadcasted_iota(jnp.int32, (tq, LANES), 1)
    for p in range(IDX_HEADS // 2):
        qp = qi_ref[0, 0, p, 0]
        qz_ref[2 * p] = jnp.where(lane < IDX_DIM, qp, jnp.zeros_like(qp))
        qz_ref[2 * p + 1] = jnp.where(lane >= IDX_DIM, qp, jnp.zeros_like(qp))
    rel = (lax.broadcasted_iota(jnp.int32, (CHUNK, tq), 0)
           - lax.broadcasted_iota(jnp.int32, (CHUNK, tq), 1))

    def score_chunk(j, _):
        kc = ki_ref[0, 0, 0, j]
        sc = jnp.zeros((CHUNK, tq), jnp.float32)
        for h in range(IDX_HEADS):
            d = lax.dot_general(kc, qz_ref[h], _NT, preferred_element_type=jnp.float32)
            sc = sc + jnp.maximum(d, 0.0) * wt_ref[0, h:h + 1, :]
        bits = pltpu.bitcast(sc, jnp.int32)
        key = bits ^ ((bits >> 31) & jnp.int32(0x7FFFFFFF))
        keys_ref[j] = jnp.where(rel <= (i - j) * CHUNK, key, jnp.int32(INT_MIN))
        return 0

    lax.fori_loop(0, i + 1, score_chunk, 0)

    def count_ge(cand):
        def body(j, acc):
            ind = jnp.where(keys_ref[j] >= cand, 1.0, 0.0)
            return acc + jnp.sum(ind.reshape(CHUNK // 8, 8, tq), axis=0)
        acc = lax.fori_loop(0, i + 1, body, jnp.zeros((8, tq), jnp.float32))
        return jnp.sum(acc, axis=0, keepdims=True)

    thr = jnp.where(count_ge(jnp.zeros((1, tq), jnp.int32)) >= topk, jnp.int32(0), jnp.int32(INT_MIN))

    def bit_body(n, thr):
        cand = thr + jnp.left_shift(jnp.int32(1), 30 - n)
        return jnp.where(count_ge(cand) >= topk, cand, thr)

    thr = lax.fori_loop(0, 31, bit_body, thr)
    thr = jnp.maximum(thr, jnp.int32(INT_MIN + 1))

    _softmax_init(m_ref, l_ref, acc_ref)

    def chunk(j, _):
        selected = keys_ref[j] >= thr

        def scores(h):
            s = lax.dot_general(k_ref[0, 0, h, j], q_ref[0, 0, h, 0], _NT, preferred_element_type=jnp.float32)
            _score_stage(h, jnp.where(selected, s, NEG_BIG), s_ref, mloc_ref)

        _staged([scores, lambda h: _softmax_stage(h, s_ref, mloc_ref, m_ref, l_ref, acc_ref, vt_ref[0, h, j])])
        return 0

    lax.fori_loop(0, i + 1, chunk, 0)
    for h in range(N_HEADS):
        o_ref[0, h] = (acc_ref[h] / l_ref[h]).astype(o_ref.dtype)


def _dsa(pq, vt, wt, topk):
    _, b, _, nc, _, _ = pq.shape
    tq = CHUNK
    kern = functools.partial(_dsa_kernel, topk=float(topk))
    return pl.pallas_call(
        kern,
        grid=(b, nc),
        in_specs=[pl.BlockSpec((1, 1, N_HEADS, 1, tq, LANES), lambda bi, i: (G_DSA_Q, bi, 0, i, 0, 0)),
                  pl.BlockSpec((1, 1, N_HEADS, nc, CHUNK, LANES), lambda bi, i: (G_DSA_K, bi, 0, 0, 0, 0)),
                  pl.BlockSpec((1, N_HEADS, nc, HEAD_DIM, CHUNK), lambda bi, i: (bi, 0, 0, 0, 0)),
                  pl.BlockSpec((1, 1, N_HEADS, 1, tq, LANES), lambda bi, i: (G_QI, bi, 0, i, 0, 0)),
                  pl.BlockSpec((1, 1, 1, nc, CHUNK, LANES), lambda bi, i: (G_V_C_KI, bi, 4, 0, 0, 0)),
                  pl.BlockSpec((1, IDX_HEADS, tq), lambda bi, i: (bi, 0, i))],
        out_specs=pl.BlockSpec((1, N_HEADS, HEAD_DIM, tq), lambda bi, i: (bi, 0, 0, i)),
        out_shape=jax.ShapeDtypeStruct((b, N_HEADS, HEAD_DIM, nc * CHUNK), jnp.bfloat16),
        scratch_shapes=[pltpu.VMEM((nc, CHUNK, tq), jnp.int32),
                        pltpu.VMEM((IDX_HEADS, tq, LANES), jnp.bfloat16)] + _softmax_scratch(tq),
        compiler_params=_params(("arbitrary", "arbitrary"), 56),
        name="dsa",
    )(pq, pq, vt, pq, pq, wt)


def _fox_kernel(q_ref, k_ref, vt_ref, o_ref, s_ref, mloc_ref, m_ref, l_ref, acc_ref):
    i = pl.program_id(1)
    tq = q_ref.shape[4]
    causal = (lax.broadcasted_iota(jnp.int32, (CHUNK, tq), 0)
              <= lax.broadcasted_iota(jnp.int32, (CHUNK, tq), 1))
    _softmax_init(m_ref, l_ref, acc_ref)

    def chunk(j, masked):
        def scores(h):
            s = lax.dot_general(k_ref[0, 0, h, j], q_ref[0, 0, h, 0], _NT, preferred_element_type=jnp.float32)
            _score_stage(h, jnp.where(causal, s, NEG_BIG) if masked else s, s_ref, mloc_ref)

        _staged([scores, lambda h: _softmax_stage(h, s_ref, mloc_ref, m_ref, l_ref, acc_ref, vt_ref[0, h, j])])
        return 0

    lax.fori_loop(0, i, lambda j, _: chunk(j, False), 0)
    chunk(i, True)
    for h in range(N_HEADS):
        o_ref[0, h] = (acc_ref[h] / l_ref[h]).astype(o_ref.dtype)


def _sb_scratch(tq):
    tile = pltpu.VMEM((N_HEADS, CHUNK, tq), jnp.bfloat16)
    vec = pltpu.VMEM((N_HEADS, 1, tq), jnp.float32)
    return [pltpu.VMEM((N_HEADS, CHUNK, tq), jnp.float32), tile, tile, tile, vec, vec,
            pltpu.VMEM((N_HEADS, HEAD_DIM, tq), jnp.float32)]


def _sb_kernel(q_ref, k_ref, vt_ref, o_ref, lb_ref, hi_ref, lo_ref, a_ref, later_ref, lom0_ref, acc_ref):
    i = pl.program_id(1)
    tq = q_ref.shape[4]
    strict = (lax.broadcasted_iota(jnp.int32, (CHUNK, tq), 0)
              < lax.broadcasted_iota(jnp.int32, (CHUNK, tq), 1))
    upper = jnp.where(lax.broadcasted_iota(jnp.int32, (CHUNK, CHUNK), 1)
                      > lax.broadcasted_iota(jnp.int32, (CHUNK, CHUNK), 0), 1.0, 0.0).astype(jnp.bfloat16)
    later_ref[...] = jnp.zeros(later_ref.shape, jnp.float32)
    acc_ref[...] = jnp.zeros(acc_ref.shape, jnp.float32)

    def chunk(j, diag):
        def logits(h):
            z = lax.dot_general(k_ref[0, 0, h, j], q_ref[0, 0, h, 0], _NT, preferred_element_type=jnp.float32)
            log_beta = _log_sigmoid(z)
            lom = log_beta - z
            if diag:
                lom = jnp.where(strict, lom, 0.0)
            hi = pltpu.bitcast(pltpu.bitcast(lom, jnp.int32) & jnp.int32(-65536), jnp.float32)
            lb_ref[h] = log_beta
            hi_ref[h] = hi.astype(jnp.bfloat16)
            lo_ref[h] = (lom - hi).astype(jnp.bfloat16)
            lom0_ref[h] = lom[0:1, :]

        def weights(h):
            within = (jnp.dot(upper, hi_ref[h], preferred_element_type=jnp.float32)
                      + jnp.dot(upper, lo_ref[h], preferred_element_type=jnp.float32))
            a = jnp.exp(lb_ref[h] + within + later_ref[h])
            if diag:
                a = jnp.where(strict, a, 0.0)
            a_ref[h] = a.astype(jnp.bfloat16)
            later_ref[h] += within[0:1, :] + lom0_ref[h]

        def values(h):
            acc_ref[h] += jnp.dot(vt_ref[0, h, j], a_ref[h], preferred_element_type=jnp.float32)

        _staged([logits, weights, values])
        return 0

    chunk(i, True)

    def live(state):
        n, later_max = state
        return jnp.logical_and(n < i, later_max > EXP_IS_ZERO_BELOW)

    def step(state):
        n, _ = state
        chunk(i - 1 - n, False)
        return n + 1, jnp.max(later_ref[...])

    lax.while_loop(live, step, (jnp.int32(0), jnp.max(later_ref[...])))
    for h in range(N_HEADS):
        o_ref[0, h] = acc_ref[h].astype(o_ref.dtype)


def _causal_attention(kernel, name, scratch, pq, vt, g_q, g_k):
    _, b, _, nc, _, _ = pq.shape
    tq = CHUNK
    return pl.pallas_call(
        kernel,
        grid=(b, nc),
        in_specs=[pl.BlockSpec((1, 1, N_HEADS, 1, tq, LANES), lambda bi, i: (g_q, bi, 0, i, 0, 0)),
                  pl.BlockSpec((1, 1, N_HEADS, nc, CHUNK, LANES), lambda bi, i: (g_k, bi, 0, 0, 0, 0)),
                  pl.BlockSpec((1, N_HEADS, nc, HEAD_DIM, CHUNK), lambda bi, i: (bi, 0, 0, 0, 0))],
        out_specs=pl.BlockSpec((1, N_HEADS, HEAD_DIM, tq), lambda bi, i: (bi, 0, 0, i)),
        out_shape=jax.ShapeDtypeStruct((b, N_HEADS, HEAD_DIM, nc * CHUNK), jnp.bfloat16),
        scratch_shapes=scratch(tq),
        compiler_params=_params(("arbitrary", "arbitrary"), 48),
        name=name,
    )(pq, pq, vt)


def _merge_kernel(oa_ref, ob_ref, oc_ref, g_ref, x_ref, wb_ref, wo_ref, gam_ref, bet_ref, o_ref, *, alpha):
    ups = [jnp.dot(o[0], wb_ref[n], preferred_element_type=jnp.float32)
           for n, o in enumerate((oa_ref, ob_ref, oc_ref))]
    cols = []
    for c in range(N_HEADS):
        sl = slice(c * LANES, (c + 1) * LANES)
        mc = ups[0][:, sl] * g_ref[0, 0, c].astype(jnp.float32)
        for n in range(1, N_BRANCHES):
            mc = mc + ups[n][:, sl] * g_ref[n, 0, c].astype(jnp.float32)
        cols.append(mc.astype(jnp.bfloat16))
    merged = jnp.concatenate(cols, axis=1)
    y = jnp.dot(merged, wo_ref[...], preferred_element_type=jnp.float32)
    o_ref[0] = _layer_norm(alpha * x_ref[0] + y, gam_ref[...], bet_ref[...])


def _merge(oa, ob, oc, pq5, x, wb, wo, gam, bet, alpha, t=512):
    b, s, d = x.shape
    o_spec = pl.BlockSpec((1, t, BRANCH), lambda bi, si: (bi, si, 0))
    return pl.pallas_call(
        functools.partial(_merge_kernel, alpha=alpha),
        grid=(b, s // t),
        in_specs=[o_spec, o_spec, o_spec,
                  pl.BlockSpec((N_BRANCHES, 1, N_HEADS, t, LANES), lambda bi, si: (G_GATES // N_BRANCHES, bi, 0, si, 0)),
                  pl.BlockSpec((1, t, d), lambda bi, si: (bi, si, 0)),
                  pl.BlockSpec((N_BRANCHES, BRANCH, d), lambda bi, si: (0, 0, 0)),
                  pl.BlockSpec((d, d), lambda bi, si: (0, 0)),
                  pl.BlockSpec((1, d), lambda bi, si: (0, 0)),
                  pl.BlockSpec((1, d), lambda bi, si: (0, 0))],
        out_specs=pl.BlockSpec((1, t, d), lambda bi, si: (bi, si, 0)),
        out_shape=jax.ShapeDtypeStruct((b, s, d), jnp.float32),
        compiler_params=_params(("arbitrary", "arbitrary"), 48),
        name="merge",
    )(oa, ob, oc, pq5, x, wb, wo, gam, bet)


def _ffn_kernel(x_ref, w1_ref, w2_ref, gam_ref, bet_ref, o_ref, *, alpha, fchunk):
    x = x_ref[0]
    xb = x.astype(jnp.bfloat16)
    y = jnp.zeros_like(x)
    for c in range(w1_ref.shape[1] // fchunk):
        sl = slice(c * fchunk, (c + 1) * fchunk)
        h = jnp.dot(xb, w1_ref[:, sl], preferred_element_type=jnp.float32)
        h = jnp.square(jnp.maximum(h, 0.0)).astype(jnp.bfloat16)
        y = y + jnp.dot(h, w2_ref[sl, :], preferred_element_type=jnp.float32)
    o_ref[0] = _layer_norm(alpha * x + y, gam_ref[...], bet_ref[...])


def _ffn(x, w1, w2, gam, bet, alpha, t=512):
    b, s, d = x.shape
    f = w1.shape[1]
    return pl.pallas_call(
        functools.partial(_ffn_kernel, alpha=alpha, fchunk=1024),
        grid=(b, s // t),
        in_specs=[pl.BlockSpec((1, t, d), lambda bi, si: (bi, si, 0)),
                  pl.BlockSpec((d, f), lambda bi, si: (0, 0)),
                  pl.BlockSpec((f, d), lambda bi, si: (0, 0)),
                  pl.BlockSpec((1, d), lambda bi, si: (0, 0)),
                  pl.BlockSpec((1, d), lambda bi, si: (0, 0))],
        out_specs=pl.BlockSpec((1, t, d), lambda bi, si: (bi, si, 0)),
        out_shape=jax.ShapeDtypeStruct((b, s, d), jnp.float32),
        compiler_params=_params(("arbitrary", "arbitrary"), 56),
        name="ffn",
    )(x, w1, w2, gam, bet)


def _pad_heads(w, n_heads, scale=1.0):
    d = w.shape[0]
    w = (w * scale).reshape(d, n_heads, HEAD_DIM)
    return jnp.pad(w, ((0, 0), (0, 0), (0, LANES - HEAD_DIM))).reshape(d, n_heads * LANES)


def _aug_weights(slopes):
    wa = np.zeros((N_GROUPS, LANES, N_HEADS * LANES), np.float32)
    for h in range(N_HEADS):
        base = h * LANES + AUG_LANE
        sl = float(slopes[h])
        wa[G_DSA_Q, XE_A, base + 0] = -sl * CHUNK
        wa[G_DSA_Q, XE_B, base + 1] = -sl
        wa[G_DSA_Q, XE_ONE, base + 2] = 1.0
        wa[G_DSA_Q, XE_ONE, base + 3] = 1.0
        wa[G_DSA_K, XE_ONE, base + 0] = 1.0
        wa[G_DSA_K, XE_ONE, base + 1] = 1.0
        wa[G_DSA_K, XE_A, base + 2] = sl * CHUNK
        wa[G_DSA_K, XE_B, base + 3] = sl
        for p in range(3):
            wa[G_FOX_Q, XE_ONE, base + p] = 1.0
            wa[G_FOX_K, XE_C + N_HEADS * p + h, base + p] = -1.0
    return wa


def _layer_weights(w_in, b_forget, slopes):
    d = w_in.shape[0]
    sizes = (BRANCH, BRANCH, BRANCH, IDX_HEADS * IDX_DIM, IDX_DIM, IDX_HEADS,
             BRANCH, BRANCH, BRANCH, N_HEADS, BRANCH, BRANCH, BRANCH, N_BRANCHES * d)
    offs = np.concatenate([[0], np.cumsum(sizes)])
    (qa, ka, va, qi, ki, wi, qf, kf, vf, fg, qs, ks, vs, gates) = [w_in[:, offs[n]:offs[n + 1]] for n in range(len(sizes))]
    scale = HEAD_DIM ** -0.5
    zeros = jnp.zeros((d, N_HEADS * LANES - BRANCH - 2 * IDX_DIM), w_in.dtype)
    groups = [_pad_heads(qa, N_HEADS, scale), _pad_heads(ka, N_HEADS),
              _pad_heads(qf, N_HEADS, scale), _pad_heads(kf, N_HEADS),
              _pad_heads(qs, N_HEADS, scale), _pad_heads(ks, N_HEADS),
              jnp.concatenate([va, vf], axis=1),
              jnp.concatenate([vs, ki, ki, zeros], axis=1),
              qi] + [gates[:, n * d:(n + 1) * d] for n in range(N_BRANCHES)]
    w_main = jnp.stack(groups).astype(jnp.bfloat16)
    w_small = jnp.concatenate([fg, wi, jnp.zeros((d, LANES - N_HEADS - IDX_HEADS), w_in.dtype)], axis=1)
    b_small = jnp.pad(b_forget, (0, LANES - N_HEADS)).reshape(1, LANES)
    return w_main, w_small.astype(jnp.bfloat16), b_small


def _v_transposed(vg, lo):
    b, _, s, _ = vg.shape
    v = vg[:, lo:lo + 4].reshape(b, 4, s // CHUNK, CHUNK, 2, HEAD_DIM)
    return v.transpose(0, 1, 4, 2, 5, 3).reshape(b, N_HEADS, s // CHUNK, HEAD_DIM, CHUNK)


def _untranspose(ot):
    b, h, dh, s = ot.shape
    return ot.transpose(0, 3, 1, 2).reshape(b, s, h * dh)


def kernel(x, w_in, b_forget, w_branch, w_out, ln1_g, ln1_b, w_ff1, w_ff2, ln2_g, ln2_b):
    depth = w_in.shape[0]
    b, s, d = x.shape
    assert s % CHUNK == 0 and d == N_HEADS * LANES
    alpha = (2.0 * depth) ** 0.25
    topk = min(TOPK_MAX, s // 4)
    slopes = 2.0 ** (-8.0 * np.arange(1, N_HEADS + 1, dtype=np.float64) / N_HEADS)
    w_aug_np = _aug_weights(slopes)
    assert np.array_equal(w_aug_np, w_aug_np.astype(jnp.bfloat16).astype(np.float32))
    w_aug = jnp.asarray(w_aug_np, jnp.bfloat16)
    nc = s // CHUNK

    for layer in range(depth):
        w_main, w_small, b_small = _layer_weights(w_in[layer], b_forget[layer], slopes)
        xb, xe, small = _prep(x, w_small, b_small)
        pq5 = _proj(xb, xe, w_main, w_aug)
        pq = pq5.reshape(N_GROUPS, b, N_HEADS, nc, CHUNK, LANES)
        wt = small[:, :, N_HEADS:N_HEADS + IDX_HEADS].transpose(0, 2, 1)
        vt_a = _v_transposed(pq5[G_V_AB], 0)
        vt_b = _v_transposed(pq5[G_V_AB], 4)
        vt_c = _v_transposed(pq5[G_V_C_KI], 0)
        o_a = _untranspose(_dsa(pq, vt_a, wt, topk))
        o_b = _untranspose(_causal_attention(_fox_kernel, "fox", _softmax_scratch, pq, vt_b, G_FOX_Q, G_FOX_K))
        o_c = _untranspose(_causal_attention(_sb_kernel, "sb", _sb_scratch, pq, vt_c, G_SB_Q, G_SB_K))
        x = _merge(o_a, o_b, o_c, pq5, x, w_branch[layer].astype(jnp.bfloat16), w_out[layer].astype(jnp.bfloat16),
                   ln1_g[layer].reshape(1, d), ln1_b[layer].reshape(1, d), alpha)
        x = _ffn(x, w_ff1[layer].astype(jnp.bfloat16), w_ff2[layer].astype(jnp.bfloat16),
                 ln2_g[layer].reshape(1, d), ln2_b[layer].reshape(1, d), alpha)
    return x
```

```python
import functools

import numpy as np
import jax
import jax.numpy as jnp
from jax import lax
from jax.experimental import pallas as pl
from jax.experimental.pallas import tpu as pltpu

HEAD_DIM = 64
N_HEADS = 8
BRANCH = N_HEADS * HEAD_DIM
IDX_HEADS = 16
IDX_DIM = 64
TOPK_MAX = 256
N_BRANCHES = 3
LN_EPS = 1e-5

LANES = 128
CHUNK = 256
N_GROUPS = 10
G_DSA_Q, G_DSA_K, G_FOX_Q, G_FOX_K, G_SB_Q, G_SB_K, G_GATES, G_QI = 0, 1, 2, 3, 4, 5, 6, 9
N_BIAS_GROUPS = 4
AUG_LANE = HEAD_DIM
XE_A, XE_B, XE_ONE, XE_C = 0, 1, 2, 3
NEG_BIG = -1e30
EXP_IS_ZERO_BELOW = -105.0
INT_MIN = -(2 ** 31)

_NT = (((1,), (1,)), ((), ()))


def _log_sigmoid(x):
    return jnp.minimum(x, 0.0) - jnp.log(1.0 + jnp.exp(-jnp.abs(x)))


def _split3(x):
    a = x.astype(jnp.bfloat16)
    r = x - a.astype(jnp.float32)
    b = r.astype(jnp.bfloat16)
    c = (r - b.astype(jnp.float32)).astype(jnp.bfloat16)
    return a, b, c


def _layer_norm(z, g, b):
    mu = jnp.mean(z, axis=-1, keepdims=True)
    zc = z - mu
    var = jnp.mean(zc * zc, axis=-1, keepdims=True)
    return zc * lax.rsqrt(var + LN_EPS) * g + b


def _params(sem, vmem_mb):
    return pltpu.CompilerParams(dimension_semantics=sem, vmem_limit_bytes=vmem_mb * 1024 * 1024)


def _prep_kernel(x_ref, ws_ref, bs_ref, xb_ref, xe_ref, sm_ref, ki_ref, carry_ref):
    s_blk = pl.program_id(1)
    t = x_ref.shape[1]
    xb = x_ref[0].astype(jnp.bfloat16)
    xb_ref[0] = xb
    both = jnp.dot(xb, ws_ref[...], preferred_element_type=jnp.float32)
    sm = both[:, :LANES]
    sm_ref[0] = sm
    ki_ref[0] = both[:, LANES:].astype(jnp.bfloat16)
    fg = sm + bs_ref[...]
    lf = jnp.minimum(fg, 0.0) - jnp.log1p(jnp.exp(-jnp.abs(fg)))

    @pl.when(s_blk == 0)
    def _():
        carry_ref[...] = jnp.zeros_like(carry_ref)

    row = lax.broadcasted_iota(jnp.int32, (t, t), 0)
    col = lax.broadcasted_iota(jnp.int32, (t, t), 1)
    tri = jnp.where(col <= row, 1.0, 0.0).astype(jnp.bfloat16)
    cs = carry_ref[0:1, :]
    for part in _split3(lf):
        cs = cs + jnp.dot(tri, part, preferred_element_type=jnp.float32)
    carry_ref[...] = jnp.broadcast_to(cs[t - 1:t, :], carry_ref.shape)

    er = lax.broadcasted_iota(jnp.int32, (LANES, LANES), 0)
    ec = lax.broadcasted_iota(jnp.int32, (LANES, LANES), 1)
    xe = jnp.zeros((t, LANES), jnp.float32)
    for p, part in enumerate(_split3(cs)):
        place = jnp.where(ec == er + (XE_C + N_HEADS * p), jnp.where(er < N_HEADS, 1.0, 0.0), 0.0)
        xe = xe + jnp.dot(part, place.astype(jnp.bfloat16), preferred_element_type=jnp.float32)
    pos = s_blk * t + lax.broadcasted_iota(jnp.int32, (t, LANES), 0)
    lane = lax.broadcasted_iota(jnp.int32, (t, LANES), 1)
    xe = jnp.where(lane == XE_A, jnp.right_shift(pos, 8).astype(jnp.float32), xe)
    xe = jnp.where(lane == XE_B, (pos & (CHUNK - 1)).astype(jnp.float32), xe)
    xe = jnp.where(lane == XE_ONE, 1.0, xe)
    xe_ref[0] = xe.astype(jnp.bfloat16)


def _prep(x, w_small, b_small, t=512):
    b, s, d = x.shape
    return pl.pallas_call(
        _prep_kernel,
        grid=(b, s // t),
        in_specs=[pl.BlockSpec((1, t, d), lambda bi, si: (bi, si, 0)),
                  pl.BlockSpec((d, 2 * LANES), lambda bi, si: (0, 0)),
                  pl.BlockSpec((1, LANES), lambda bi, si: (0, 0))],
        out_specs=[pl.BlockSpec((1, t, d), lambda bi, si: (bi, si, 0)),
                   pl.BlockSpec((1, t, LANES), lambda bi, si: (bi, si, 0)),
                   pl.BlockSpec((1, t, LANES), lambda bi, si: (bi, si, 0)),
                   pl.BlockSpec((1, t, LANES), lambda bi, si: (bi, si, 0))],
        out_shape=[jax.ShapeDtypeStruct((b, s, d), jnp.bfloat16),
                   jax.ShapeDtypeStruct((b, s, LANES), jnp.bfloat16),
                   jax.ShapeDtypeStruct((b, s, LANES), jnp.float32),
                   jax.ShapeDtypeStruct((b, s, LANES), jnp.bfloat16)],
        scratch_shapes=[pltpu.VMEM((8, LANES), jnp.float32)],
        compiler_params=_params(("arbitrary", "arbitrary"), 32),
        name="prep",
    )(x, w_small, b_small)


def _proj_kernel(xb_ref, xe_ref, w_ref, wa_ref, o_ref):
    j = pl.program_id(0)

    def column_blocks(with_bias):
        is_gate = jnp.logical_and(j >= G_GATES, j < G_GATES + N_BRANCHES)
        for c in range(N_HEADS * LANES // CHUNK):
            sl = slice(c * CHUNK, (c + 1) * CHUNK)
            acc = jnp.dot(xb_ref[0], w_ref[0, :, sl], preferred_element_type=jnp.float32)
            if with_bias:
                acc = acc + jnp.dot(xe_ref[0], wa_ref[0, :, sl], preferred_element_type=jnp.float32)
            else:
                acc = jnp.where(is_gate, jax.nn.sigmoid(acc), acc)
            o_ref[0, 0, 2 * c] = acc[:, :LANES].astype(jnp.bfloat16)
            o_ref[0, 0, 2 * c + 1] = acc[:, LANES:].astype(jnp.bfloat16)

    @pl.when(j < N_BIAS_GROUPS)
    def _():
        column_blocks(True)

    @pl.when(j >= N_BIAS_GROUPS)
    def _():
        column_blocks(False)


def _proj(xb, xe, w_main, w_aug, t=1024):
    b, s, d = xb.shape
    ns = s // t
    return pl.pallas_call(
        _proj_kernel,
        grid=(N_GROUPS, b * ns),
        in_specs=[pl.BlockSpec((1, t, d), lambda j, i: (i // ns, i % ns, 0)),
                  pl.BlockSpec((1, t, LANES), lambda j, i: (i // ns, i % ns, 0)),
                  pl.BlockSpec((1, d, N_HEADS * LANES), lambda j, i: (j, 0, 0)),
                  pl.BlockSpec((1, LANES, N_HEADS * LANES), lambda j, i: (jnp.minimum(j, N_BIAS_GROUPS - 1), 0, 0))],
        out_specs=pl.BlockSpec((1, 1, N_HEADS, t, LANES), lambda j, i: (j, i // ns, 0, i % ns, 0)),
        out_shape=jax.ShapeDtypeStruct((N_GROUPS, b, N_HEADS, s, LANES), jnp.bfloat16),
        compiler_params=_params(("arbitrary", "arbitrary"), 48),
        name="proj",
    )(xb, xe, w_main, w_aug)


def _proj_vt_kernel(xb_ref, w_ref, o_ref):
    vt = lax.dot_general(w_ref[...], xb_ref[0], _NT, preferred_element_type=jnp.float32)
    for c in range(xb_ref.shape[1] // CHUNK):
        blk = vt[:, c * CHUNK:(c + 1) * CHUNK].astype(jnp.bfloat16)
        o_ref[0, :, c] = blk.reshape(N_BRANCHES * N_HEADS, HEAD_DIM, CHUNK)


def _proj_vt(xb, w_vt, t=1024):
    b, s, d = xb.shape
    nh = N_BRANCHES * N_HEADS
    return pl.pallas_call(
        _proj_vt_kernel,
        grid=(b, s // t),
        in_specs=[pl.BlockSpec((1, t, d), lambda bi, si: (bi, si, 0)),
                  pl.BlockSpec((nh * HEAD_DIM, d), lambda bi, si: (0, 0))],
        out_specs=pl.BlockSpec((1, nh, t // CHUNK, HEAD_DIM, CHUNK), lambda bi, si: (bi, 0, si, 0, 0)),
        out_shape=jax.ShapeDtypeStruct((b, nh, s // CHUNK, HEAD_DIM, CHUNK), jnp.bfloat16),
        compiler_params=_params(("arbitrary", "arbitrary"), 48),
        name="proj_vt",
    )(xb, w_vt)


STAGE_LAG = 8
COUNT_ACCS = 4


def _staged(stages):
    for step in range(N_HEADS + STAGE_LAG * (len(stages) - 1)):
        for n, stage in enumerate(stages):
            if 0 <= step - STAGE_LAG * n < N_HEADS:
                stage(step - STAGE_LAG * n)


def _softmax_tile(j0, n, q_ref, k_ref, vt_ref, mask, refs):
    s_ref, mloc_ref, m_ref, l_ref, acc_ref = refs
    rows = n * CHUNK

    def scores(h):
        keys = k_ref[0, 0, h, pl.ds(j0, n)].reshape(rows, LANES)
        s = mask(lax.dot_general(keys, q_ref[0, 0, h, 0], _NT, preferred_element_type=jnp.float32))
        s_ref[h, :rows] = s
        mloc_ref[h] = jnp.max(s, axis=0, keepdims=True)

    def update(h):
        m = m_ref[h]
        m_new = jnp.maximum(m, mloc_ref[h])
        alpha = jnp.exp(m - m_new)
        p = jnp.exp(s_ref[h, :rows] - m_new)
        m_ref[h] = m_new
        l_ref[h] = alpha * l_ref[h] + jnp.sum(p, axis=0, keepdims=True)
        pb = p.astype(jnp.bfloat16)
        pv = jnp.dot(vt_ref[0, h, j0], pb[:CHUNK], preferred_element_type=jnp.float32)
        for c in range(1, n):
            pv = pv + jnp.dot(vt_ref[0, h, j0 + c], pb[c * CHUNK:(c + 1) * CHUNK], preferred_element_type=jnp.float32)
        acc_ref[h] = alpha * acc_ref[h] + pv

    _staged([scores, update])


def _store_heads(o_ref, head_out):
    for p in range(N_HEADS // 2):
        pair = jnp.concatenate([head_out(2 * p), head_out(2 * p + 1)], axis=0)
        o_ref[0, :, p * LANES:(p + 1) * LANES] = pair.T.astype(o_ref.dtype)


def _softmax_init(m_ref, l_ref, acc_ref):
    m_ref[...] = jnp.full(m_ref.shape, NEG_BIG, jnp.float32)
    l_ref[...] = jnp.zeros(l_ref.shape, jnp.float32)
    acc_ref[...] = jnp.zeros(acc_ref.shape, jnp.float32)


TILE_CHUNKS = 2


def _softmax_scratch(tq):
    vec = pltpu.VMEM((N_HEADS, 1, tq), jnp.float32)
    return [pltpu.VMEM((N_HEADS, TILE_CHUNKS * CHUNK, tq), jnp.float32), vec, vec, vec,
            pltpu.VMEM((N_HEADS, HEAD_DIM, tq), jnp.float32)]


def _softmax_chunks(n_chunks, tile):
    n_tiles = n_chunks // TILE_CHUNKS

    def body(t, _):
        tile(t * TILE_CHUNKS, TILE_CHUNKS)
        return 0

    lax.fori_loop(0, n_tiles, body, 0)
    done = n_tiles * TILE_CHUNKS
    for r in range(1, TILE_CHUNKS):
        @pl.when(n_chunks - done == r)
        def _():
            for c in range(r):
                tile(done + c, 1)


def _dsa_kernel(q_ref, k_ref, vt_ref, qi_ref, ki_ref, wt_ref, o_ref, keys_ref, qz_ref,
                s_ref, mloc_ref, m_ref, l_ref, acc_ref, *, topk):
    i = pl.program_id(1)
    tq = q_ref.shape[4]
    lane = lax.broadcasted_iota(jnp.int32, (tq, LANES), 1)
    for p in range(IDX_HEADS // 2):
        qp = qi_ref[0, 0, p, 0]
        qz_ref[2 * p] = jnp.where(lane < IDX_DIM, qp, jnp.zeros_like(qp))
        qz_ref[2 * p + 1] = jnp.where(lane >= IDX_DIM, qp, jnp.zeros_like(qp))
    rel = (lax.broadcasted_iota(jnp.int32, (CHUNK, tq), 0)
           - lax.broadcasted_iota(jnp.int32, (CHUNK, tq), 1))

    def score_chunk(j, _):
        kc = ki_ref[0, j]
        sc = jnp.zeros((CHUNK, tq), jnp.float32)
        for h in range(IDX_HEADS):
            d = lax.dot_general(kc, qz_ref[h], _NT, preferred_element_type=jnp.float32)
            sc = sc + jnp.maximum(d, 0.0) * wt_ref[0, h:h + 1, :]
        bits = pltpu.bitcast(sc, jnp.int32)
        key = bits ^ ((bits >> 31) & jnp.int32(0x7FFFFFFF))
        keys_ref[j] = jnp.where(rel <= (i - j) * CHUNK, key, jnp.int32(INT_MIN))
        return 0

    lax.fori_loop(0, i + 1, score_chunk, 0)

    def count_ge(cand):
        def body(j, acc):
            ind = jnp.where(keys_ref[j] >= cand, 1.0, 0.0)
            return acc + jnp.sum(ind.reshape(CHUNK // (8 * COUNT_ACCS), COUNT_ACCS * 8, tq), axis=0)
        acc = lax.fori_loop(0, i + 1, body, jnp.zeros((COUNT_ACCS * 8, tq), jnp.float32))
        return jnp.sum(acc, axis=0, keepdims=True)

    thr = jnp.where(count_ge(jnp.zeros((1, tq), jnp.int32)) >= topk, jnp.int32(0), jnp.int32(INT_MIN))

    def bit_body(n, thr):
        cand = thr + jnp.left_shift(jnp.int32(1), 30 - n)
        return jnp.where(count_ge(cand) >= topk, cand, thr)

    thr = lax.fori_loop(0, 31, bit_body, thr)
    thr = jnp.maximum(thr, jnp.int32(INT_MIN + 1))

    _softmax_init(m_ref, l_ref, acc_ref)

    def tile(j0, n):
        selected = keys_ref[pl.ds(j0, n)].reshape(n * CHUNK, tq) >= thr
        _softmax_tile(j0, n, q_ref, k_ref, vt_ref, lambda s: jnp.where(selected, s, NEG_BIG),
                      (s_ref, mloc_ref, m_ref, l_ref, acc_ref))

    _softmax_chunks(i + 1, tile)
    _store_heads(o_ref, lambda h: acc_ref[h] / l_ref[h])


def _dsa(pq, vt, ki, wt, topk):
    _, b, _, nc, _, _ = pq.shape
    tq = CHUNK
    kern = functools.partial(_dsa_kernel, topk=float(topk))
    return pl.pallas_call(
        kern,
        grid=(b, nc),
        in_specs=[pl.BlockSpec((1, 1, N_HEADS, 1, tq, LANES), lambda bi, i: (G_DSA_Q, bi, 0, i, 0, 0)),
                  pl.BlockSpec((1, 1, N_HEADS, nc, CHUNK, LANES), lambda bi, i: (G_DSA_K, bi, 0, 0, 0, 0)),
                  pl.BlockSpec((1, N_HEADS, nc, HEAD_DIM, CHUNK), lambda bi, i: (bi, 0, 0, 0, 0)),
                  pl.BlockSpec((1, 1, N_HEADS, 1, tq, LANES), lambda bi, i: (G_QI, bi, 0, i, 0, 0)),
                  pl.BlockSpec((1, nc, CHUNK, LANES), lambda bi, i: (bi, 0, 0, 0)),
                  pl.BlockSpec((1, IDX_HEADS, tq), lambda bi, i: (bi, 0, i))],
        out_specs=pl.BlockSpec((1, tq, BRANCH), lambda bi, i: (bi, i, 0)),
        out_shape=jax.ShapeDtypeStruct((b, nc * CHUNK, BRANCH), jnp.bfloat16),
        scratch_shapes=[pltpu.VMEM((nc, CHUNK, tq), jnp.int32),
                        pltpu.VMEM((IDX_HEADS, tq, LANES), jnp.bfloat16)] + _softmax_scratch(tq),
        compiler_params=_params(("arbitrary", "arbitrary"), 56),
        name="dsa",
    )(pq, pq, vt, pq, ki, wt)


def _fox_kernel(q_ref, k_ref, vt_ref, o_ref, s_ref, mloc_ref, m_ref, l_ref, acc_ref):
    i = pl.program_id(1)
    tq = q_ref.shape[4]
    causal = (lax.broadcasted_iota(jnp.int32, (CHUNK, tq), 0)
              <= lax.broadcasted_iota(jnp.int32, (CHUNK, tq), 1))
    _softmax_init(m_ref, l_ref, acc_ref)
    refs = (s_ref, mloc_ref, m_ref, l_ref, acc_ref)
    _softmax_chunks(i, lambda j0, n: _softmax_tile(j0, n, q_ref, k_ref, vt_ref, lambda s: s, refs))
    _softmax_tile(i, 1, q_ref, k_ref, vt_ref, lambda s: jnp.where(causal, s, NEG_BIG), refs)
    _store_heads(o_ref, lambda h: acc_ref[h] / l_ref[h])


def _sb_scratch(tq):
    tile = pltpu.VMEM((N_HEADS, CHUNK, tq), jnp.bfloat16)
    vec = pltpu.VMEM((N_HEADS, 1, tq), jnp.float32)
    return [pltpu.VMEM((N_HEADS, CHUNK, tq), jnp.float32), tile, tile, tile, vec, vec,
            pltpu.VMEM((N_HEADS, HEAD_DIM, tq), jnp.float32)]


def _sb_kernel(q_ref, k_ref, vt_ref, o_ref, lb_ref, hi_ref, lo_ref, a_ref, later_ref, lom0_ref, acc_ref):
    i = pl.program_id(1)
    tq = q_ref.shape[4]
    strict = (lax.broadcasted_iota(jnp.int32, (CHUNK, tq), 0)
              < lax.broadcasted_iota(jnp.int32, (CHUNK, tq), 1))
    upper = jnp.where(lax.broadcasted_iota(jnp.int32, (CHUNK, CHUNK), 1)
                      > lax.broadcasted_iota(jnp.int32, (CHUNK, CHUNK), 0), 1.0, 0.0).astype(jnp.bfloat16)
    later_ref[...] = jnp.zeros(later_ref.shape, jnp.float32)
    acc_ref[...] = jnp.zeros(acc_ref.shape, jnp.float32)

    def chunk(j, diag):
        def logits(h):
            z = lax.dot_general(k_ref[0, 0, h, j], q_ref[0, 0, h, 0], _NT, preferred_element_type=jnp.float32)
            log_beta = _log_sigmoid(z)
            lom = log_beta - z
            if diag:
                lom = jnp.where(strict, lom, 0.0)
            hi = pltpu.bitcast(pltpu.bitcast(lom, jnp.int32) & jnp.int32(-65536), jnp.float32)
            lb_ref[h] = log_beta
            hi_ref[h] = hi.astype(jnp.bfloat16)
            lo_ref[h] = (lom - hi).astype(jnp.bfloat16)
            lom0_ref[h] = lom[0:1, :]

        def weights(h):
            within = (jnp.dot(upper, hi_ref[h], preferred_element_type=jnp.float32)
                      + jnp.dot(upper, lo_ref[h], preferred_element_type=jnp.float32))
            a = jnp.exp(lb_ref[h] + within + later_ref[h])
            if diag:
                a = jnp.where(strict, a, 0.0)
            a_ref[h] = a.astype(jnp.bfloat16)
            later_ref[h] += within[0:1, :] + lom0_ref[h]

        def values(h):
            acc_ref[h] += jnp.dot(vt_ref[0, h, j], a_ref[h], preferred_element_type=jnp.float32)

        _staged([logits, weights, values])
        return 0

    chunk(i, True)

    def live(state):
        n, later_max = state
        return jnp.logical_and(n < i, later_max > EXP_IS_ZERO_BELOW)

    def step(state):
        n, _ = state
        chunk(i - 1 - n, False)
        return n + 1, jnp.max(later_ref[...])

    lax.while_loop(live, step, (jnp.int32(0), jnp.max(later_ref[...])))
    _store_heads(o_ref, lambda h: acc_ref[h])


def _causal_attention(kernel, name, scratch, pq, vt, g_q, g_k, branch):
    _, b, _, nc, _, _ = pq.shape
    tq = CHUNK
    return pl.pallas_call(
        kernel,
        grid=(b, nc),
        in_specs=[pl.BlockSpec((1, 1, N_HEADS, 1, tq, LANES), lambda bi, i: (g_q, bi, 0, i, 0, 0)),
                  pl.BlockSpec((1, 1, N_HEADS, nc, CHUNK, LANES), lambda bi, i: (g_k, bi, 0, 0, 0, 0)),
                  pl.BlockSpec((1, N_HEADS, nc, HEAD_DIM, CHUNK), lambda bi, i: (bi, branch, 0, 0, 0))],
        out_specs=pl.BlockSpec((1, tq, BRANCH), lambda bi, i: (bi, i, 0)),
        out_shape=jax.ShapeDtypeStruct((b, nc * CHUNK, BRANCH), jnp.bfloat16),
        scratch_shapes=scratch(tq),
        compiler_params=_params(("arbitrary", "arbitrary"), 48),
        name=name,
    )(pq, pq, vt)


def _merge_kernel(oa_ref, ob_ref, oc_ref, g_ref, x_ref, wb_ref, wo_ref, gam_ref, bet_ref, o_ref, *, alpha):
    ups = [jnp.dot(o[0], wb_ref[n], preferred_element_type=jnp.float32)
           for n, o in enumerate((oa_ref, ob_ref, oc_ref))]
    cols = []
    for c in range(N_HEADS):
        sl = slice(c * LANES, (c + 1) * LANES)
        mc = ups[0][:, sl] * g_ref[0, 0, c].astype(jnp.float32)
        for n in range(1, N_BRANCHES):
            mc = mc + ups[n][:, sl] * g_ref[n, 0, c].astype(jnp.float32)
        cols.append(mc.astype(jnp.bfloat16))
    merged = jnp.concatenate(cols, axis=1)
    y = jnp.dot(merged, wo_ref[...], preferred_element_type=jnp.float32)
    o_ref[0] = _layer_norm(alpha * x_ref[0] + y, gam_ref[...], bet_ref[...])


def _merge(oa, ob, oc, pq5, x, wb, wo, gam, bet, alpha, t=512):
    b, s, d = x.shape
    o_spec = pl.BlockSpec((1, t, BRANCH), lambda bi, si: (bi, si, 0))
    return pl.pallas_call(
        functools.partial(_merge_kernel, alpha=alpha),
        grid=(b, s // t),
        in_specs=[o_spec, o_spec, o_spec,
                  pl.BlockSpec((N_BRANCHES, 1, N_HEADS, t, LANES), lambda bi, si: (G_GATES // N_BRANCHES, bi, 0, si, 0)),
                  pl.BlockSpec((1, t, d), lambda bi, si: (bi, si, 0)),
                  pl.BlockSpec((N_BRANCHES, BRANCH, d), lambda bi, si: (0, 0, 0)),
                  pl.BlockSpec((d, d), lambda bi, si: (0, 0)),
                  pl.BlockSpec((1, d), lambda bi, si: (0, 0)),
                  pl.BlockSpec((1, d), lambda bi, si: (0, 0))],
        out_specs=pl.BlockSpec((1, t, d), lambda bi, si: (bi, si, 0)),
        out_shape=jax.ShapeDtypeStruct((b, s, d), jnp.float32),
        compiler_params=_params(("arbitrary", "arbitrary"), 48),
        name="merge",
    )(oa, ob, oc, pq5, x, wb, wo, gam, bet)


def _ffn_kernel(x_ref, w1_ref, w2_ref, gam_ref, bet_ref, o_ref, *, alpha, fchunk):
    x = x_ref[0]
    xb = x.astype(jnp.bfloat16)
    y = jnp.zeros_like(x)
    for c in range(w1_ref.shape[1] // fchunk):
        sl = slice(c * fchunk, (c + 1) * fchunk)
        h = jnp.dot(xb, w1_ref[:, sl], preferred_element_type=jnp.float32)
        h = jnp.square(jnp.maximum(h, 0.0)).astype(jnp.bfloat16)
        y = y + jnp.dot(h, w2_ref[sl, :], preferred_element_type=jnp.float32)
    o_ref[0] = _layer_norm(alpha * x + y, gam_ref[...], bet_ref[...])


def _ffn(x, w1, w2, gam, bet, alpha, t=512):
    b, s, d = x.shape
    f = w1.shape[1]
    return pl.pallas_call(
        functools.partial(_ffn_kernel, alpha=alpha, fchunk=1024),
        grid=(b, s // t),
        in_specs=[pl.BlockSpec((1, t, d), lambda bi, si: (bi, si, 0)),
                  pl.BlockSpec((d, f), lambda bi, si: (0, 0)),
                  pl.BlockSpec((f, d), lambda bi, si: (0, 0)),
                  pl.BlockSpec((1, d), lambda bi, si: (0, 0)),
                  pl.BlockSpec((1, d), lambda bi, si: (0, 0))],
        out_specs=pl.BlockSpec((1, t, d), lambda bi, si: (bi, si, 0)),
        out_shape=jax.ShapeDtypeStruct((b, s, d), jnp.float32),
        compiler_params=_params(("arbitrary", "arbitrary"), 56),
        name="ffn",
    )(x, w1, w2, gam, bet)


def _pad_heads(w, n_heads, scale=1.0):
    d = w.shape[0]
    w = (w * scale).reshape(d, n_heads, HEAD_DIM)
    return jnp.pad(w, ((0, 0), (0, 0), (0, LANES - HEAD_DIM))).reshape(d, n_heads * LANES)


def _aug_weights(slopes):
    wa = np.zeros((N_BIAS_GROUPS, LANES, N_HEADS * LANES), np.float32)
    for h in range(N_HEADS):
        base = h * LANES + AUG_LANE
        sl = float(slopes[h])
        wa[G_DSA_Q, XE_A, base + 0] = -sl * CHUNK
        wa[G_DSA_Q, XE_B, base + 1] = -sl
        wa[G_DSA_Q, XE_ONE, base + 2] = 1.0
        wa[G_DSA_Q, XE_ONE, base + 3] = 1.0
        wa[G_DSA_K, XE_ONE, base + 0] = 1.0
        wa[G_DSA_K, XE_ONE, base + 1] = 1.0
        wa[G_DSA_K, XE_A, base + 2] = sl * CHUNK
        wa[G_DSA_K, XE_B, base + 3] = sl
        for p in range(3):
            wa[G_FOX_Q, XE_ONE, base + p] = 1.0
            wa[G_FOX_K, XE_C + N_HEADS * p + h, base + p] = -1.0
    return wa


def _layer_weights(w_in, b_forget, slopes):
    d = w_in.shape[0]
    sizes = (BRANCH, BRANCH, BRANCH, IDX_HEADS * IDX_DIM, IDX_DIM, IDX_HEADS,
             BRANCH, BRANCH, BRANCH, N_HEADS, BRANCH, BRANCH, BRANCH, N_BRANCHES * d)
    offs = np.concatenate([[0], np.cumsum(sizes)])
    (qa, ka, va, qi, ki, wi, qf, kf, vf, fg, qs, ks, vs, gates) = [w_in[:, offs[n]:offs[n + 1]] for n in range(len(sizes))]
    scale = HEAD_DIM ** -0.5
    groups = [_pad_heads(qa, N_HEADS, scale), _pad_heads(ka, N_HEADS),
              _pad_heads(qf, N_HEADS, scale), _pad_heads(kf, N_HEADS),
              _pad_heads(qs, N_HEADS, scale), _pad_heads(ks, N_HEADS)]
    groups += [gates[:, n * d:(n + 1) * d] for n in range(N_BRANCHES)] + [qi]
    w_main = jnp.stack(groups).astype(jnp.bfloat16)
    w_small = jnp.concatenate([fg, wi, jnp.zeros((d, LANES - N_HEADS - IDX_HEADS), w_in.dtype), ki, ki], axis=1)
    b_small = jnp.pad(b_forget, (0, LANES - N_HEADS)).reshape(1, LANES)
    w_vt = jnp.concatenate([va, vf, vs], axis=1).T.astype(jnp.bfloat16)
    return w_main, w_small.astype(jnp.bfloat16), b_small, w_vt


def kernel(x, w_in, b_forget, w_branch, w_out, ln1_g, ln1_b, w_ff1, w_ff2, ln2_g, ln2_b):
    depth = w_in.shape[0]
    b, s, d = x.shape
    assert s % CHUNK == 0 and d == N_HEADS * LANES
    alpha = (2.0 * depth) ** 0.25
    topk = min(TOPK_MAX, s // 4)
    slopes = 2.0 ** (-8.0 * np.arange(1, N_HEADS + 1, dtype=np.float64) / N_HEADS)
    w_aug_np = _aug_weights(slopes)
    assert np.array_equal(w_aug_np, w_aug_np.astype(jnp.bfloat16).astype(np.float32))
    w_aug = jnp.asarray(w_aug_np, jnp.bfloat16)
    nc = s // CHUNK

    for layer in range(depth):
        w_main, w_small, b_small, w_vt = _layer_weights(w_in[layer], b_forget[layer], slopes)
        xb, xe, small, ki = _prep(x, w_small, b_small)
        pq5 = _proj(xb, xe, w_main, w_aug)
        pq = pq5.reshape(N_GROUPS, b, N_HEADS, nc, CHUNK, LANES)
        vt = _proj_vt(xb, w_vt)
        wt = small[:, :, N_HEADS:N_HEADS + IDX_HEADS].transpose(0, 2, 1)
        o_a = _dsa(pq, vt, ki.reshape(b, nc, CHUNK, LANES), wt, topk)
        o_b = _causal_attention(_fox_kernel, "fox", _softmax_scratch, pq, vt, G_FOX_Q, G_FOX_K, 1)
        o_c = _causal_attention(_sb_kernel, "sb", _sb_scratch, pq, vt, G_SB_Q, G_SB_K, 2)
        x = _merge(o_a, o_b, o_c, pq5, x, w_branch[layer].astype(jnp.bfloat16), w_out[layer].astype(jnp.bfloat16),
                   ln1_g[layer].reshape(1, d), ln1_b[layer].reshape(1, d), alpha)
        x = _ffn(x, w_ff1[layer].astype(jnp.bfloat16), w_ff2[layer].astype(jnp.bfloat16),
                 ln2_g[layer].reshape(1, d), ln2_b[layer].reshape(1, d), alpha)
    return x
```

```python
import functools

import numpy as np
import jax
import jax.numpy as jnp
from jax import lax
from jax.experimental import pallas as pl
from jax.experimental.pallas import tpu as pltpu

HEAD_DIM = 64
N_HEADS = 8
BRANCH = N_HEADS * HEAD_DIM
IDX_HEADS = 16
IDX_DIM = 64
TOPK_MAX = 256
N_BRANCHES = 3
LN_EPS = 1e-5

LANES = 128
CHUNK = 256
N_GROUPS = 9
G_DSA_Q, G_DSA_K, G_FOX_Q, G_FOX_K, G_SB, G_QI, G_GATES = 0, 1, 2, 3, 4, 5, 6
N_BIAS_GROUPS = 4
AUG_LANE = HEAD_DIM
XE_A, XE_B, XE_ONE, XE_C = 0, 1, 2, 3
NEG_BIG = -1e30
EXP_IS_ZERO_BELOW = -105.0
INT_MIN = -(2 ** 31)
I16_MIN = -(2 ** 15)

_NT = (((1,), (1,)), ((), ()))


def _log_sigmoid(x):
    return jnp.minimum(x, 0.0) - jnp.log(1.0 + jnp.exp(-jnp.abs(x)))


def _split3(x):
    a = x.astype(jnp.bfloat16)
    r = x - a.astype(jnp.float32)
    b = r.astype(jnp.bfloat16)
    c = (r - b.astype(jnp.float32)).astype(jnp.bfloat16)
    return a, b, c


def _layer_norm(z, g, b):
    mu = jnp.mean(z, axis=-1, keepdims=True)
    zc = z - mu
    var = jnp.mean(zc * zc, axis=-1, keepdims=True)
    return zc * lax.rsqrt(var + LN_EPS) * g + b


def _params(sem, vmem_mb):
    return pltpu.CompilerParams(dimension_semantics=sem, vmem_limit_bytes=vmem_mb * 1024 * 1024)


def _prep_kernel(x_ref, ws_ref, bs_ref, xb_ref, xe_ref, sm_ref, ki_ref, carry_ref):
    s_blk = pl.program_id(1)
    t = x_ref.shape[1]
    xb = x_ref[0].astype(jnp.bfloat16)
    xb_ref[0] = xb
    both = jnp.dot(xb, ws_ref[...], preferred_element_type=jnp.float32)
    sm = both[:, :LANES]
    sm_ref[0] = sm
    ki_ref[0] = both[:, LANES:].astype(jnp.bfloat16)
    fg = sm + bs_ref[...]
    lf = jnp.minimum(fg, 0.0) - jnp.log1p(jnp.exp(-jnp.abs(fg)))

    @pl.when(s_blk == 0)
    def _():
        carry_ref[...] = jnp.zeros_like(carry_ref)

    row = lax.broadcasted_iota(jnp.int32, (t, t), 0)
    col = lax.broadcasted_iota(jnp.int32, (t, t), 1)
    tri = jnp.where(col <= row, 1.0, 0.0).astype(jnp.bfloat16)
    cs = carry_ref[0:1, :]
    for part in _split3(lf):
        cs = cs + jnp.dot(tri, part, preferred_element_type=jnp.float32)
    carry_ref[...] = jnp.broadcast_to(cs[t - 1:t, :], carry_ref.shape)

    er = lax.broadcasted_iota(jnp.int32, (LANES, LANES), 0)
    ec = lax.broadcasted_iota(jnp.int32, (LANES, LANES), 1)
    xe = jnp.zeros((t, LANES), jnp.float32)
    for p, part in enumerate(_split3(cs)):
        place = jnp.where(ec == er + (XE_C + N_HEADS * p), jnp.where(er < N_HEADS, 1.0, 0.0), 0.0)
        xe = xe + jnp.dot(part, place.astype(jnp.bfloat16), preferred_element_type=jnp.float32)
    pos = s_blk * t + lax.broadcasted_iota(jnp.int32, (t, LANES), 0)
    lane = lax.broadcasted_iota(jnp.int32, (t, LANES), 1)
    xe = jnp.where(lane == XE_A, jnp.right_shift(pos, 8).astype(jnp.float32), xe)
    xe = jnp.where(lane == XE_B, (pos & (CHUNK - 1)).astype(jnp.float32), xe)
    xe = jnp.where(lane == XE_ONE, 1.0, xe)
    xe_ref[0] = xe.astype(jnp.bfloat16)


def _prep(x, w_small, b_small, t=512):
    b, s, d = x.shape
    return pl.pallas_call(
        _prep_kernel,
        grid=(b, s // t),
        in_specs=[pl.BlockSpec((1, t, d), lambda bi, si: (bi, si, 0)),
                  pl.BlockSpec((d, 2 * LANES), lambda bi, si: (0, 0)),
                  pl.BlockSpec((1, LANES), lambda bi, si: (0, 0))],
        out_specs=[pl.BlockSpec((1, t, d), lambda bi, si: (bi, si, 0)),
                   pl.BlockSpec((1, t, LANES), lambda bi, si: (bi, si, 0)),
                   pl.BlockSpec((1, t, LANES), lambda bi, si: (bi, si, 0)),
                   pl.BlockSpec((1, t, LANES), lambda bi, si: (bi, si, 0))],
        out_shape=[jax.ShapeDtypeStruct((b, s, d), jnp.bfloat16),
                   jax.ShapeDtypeStruct((b, s, LANES), jnp.bfloat16),
                   jax.ShapeDtypeStruct((b, s, LANES), jnp.float32),
                   jax.ShapeDtypeStruct((b, s, LANES), jnp.bfloat16)],
        scratch_shapes=[pltpu.VMEM((8, LANES), jnp.float32)],
        compiler_params=_params(("arbitrary", "arbitrary"), 32),
        name="prep",
    )(x, w_small, b_small)


def _proj_kernel(xb_ref, xe_ref, w_ref, wa_ref, o_ref):
    j = pl.program_id(0)

    def column_blocks(with_bias):
        is_gate = jnp.logical_and(j >= G_GATES, j < G_GATES + N_BRANCHES)
        for c in range(N_HEADS * LANES // CHUNK):
            sl = slice(c * CHUNK, (c + 1) * CHUNK)
            acc = jnp.dot(xb_ref[0], w_ref[0, :, sl], preferred_element_type=jnp.float32)
            if with_bias:
                acc = acc + jnp.dot(xe_ref[0], wa_ref[0, :, sl], preferred_element_type=jnp.float32)
            else:
                acc = jnp.where(is_gate, jax.nn.sigmoid(acc), acc)
            o_ref[0, 0, 2 * c] = acc[:, :LANES].astype(jnp.bfloat16)
            o_ref[0, 0, 2 * c + 1] = acc[:, LANES:].astype(jnp.bfloat16)

    @pl.when(j < N_BIAS_GROUPS)
    def _():
        column_blocks(True)

    @pl.when(j >= N_BIAS_GROUPS)
    def _():
        column_blocks(False)


def _proj(xb, xe, w_main, w_aug, t=1024):
    b, s, d = xb.shape
    assert s % t == 0
    ns = s // t
    return pl.pallas_call(
        _proj_kernel,
        grid=(N_GROUPS, b * ns),
        in_specs=[pl.BlockSpec((1, t, d), lambda j, i: (i // ns, i % ns, 0)),
                  pl.BlockSpec((1, t, LANES), lambda j, i: (i // ns, i % ns, 0)),
                  pl.BlockSpec((1, d, N_HEADS * LANES), lambda j, i: (j, 0, 0)),
                  pl.BlockSpec((1, LANES, N_HEADS * LANES), lambda j, i: (jnp.minimum(j, N_BIAS_GROUPS - 1), 0, 0))],
        out_specs=pl.BlockSpec((1, 1, N_HEADS, t, LANES), lambda j, i: (j, i // ns, 0, i % ns, 0)),
        out_shape=jax.ShapeDtypeStruct((N_GROUPS, b, N_HEADS, s, LANES), jnp.bfloat16),
        compiler_params=_params(("arbitrary", "arbitrary"), 48),
        name="proj",
    )(xb, xe, w_main, w_aug)


def _proj_vt_kernel(xb_ref, w_ref, o_ref):
    vt = lax.dot_general(w_ref[...], xb_ref[0], _NT, preferred_element_type=jnp.float32)
    for c in range(xb_ref.shape[1] // CHUNK):
        blk = vt[:, c * CHUNK:(c + 1) * CHUNK].astype(jnp.bfloat16)
        o_ref[0, :, c] = blk.reshape(N_BRANCHES * N_HEADS, HEAD_DIM, CHUNK)


def _proj_vt(xb, w_vt, t=1024):
    b, s, d = xb.shape
    nh = N_BRANCHES * N_HEADS
    return pl.pallas_call(
        _proj_vt_kernel,
        grid=(b, s // t),
        in_specs=[pl.BlockSpec((1, t, d), lambda bi, si: (bi, si, 0)),
                  pl.BlockSpec((nh * HEAD_DIM, d), lambda bi, si: (0, 0))],
        out_specs=pl.BlockSpec((1, nh, t // CHUNK, HEAD_DIM, CHUNK), lambda bi, si: (bi, 0, si, 0, 0)),
        out_shape=jax.ShapeDtypeStruct((b, nh, s // CHUNK, HEAD_DIM, CHUNK), jnp.bfloat16),
        compiler_params=_params(("arbitrary", "arbitrary"), 48),
        name="proj_vt",
    )(xb, w_vt)


STAGE_LAG = 8
COUNT_ACCS = 4


def _staged(stages):
    for step in range(N_HEADS + STAGE_LAG * (len(stages) - 1)):
        for n, stage in enumerate(stages):
            if 0 <= step - STAGE_LAG * n < N_HEADS:
                stage(step - STAGE_LAG * n)


def _softmax_tile(j0, n, q_ref, k_ref, vt_ref, mask, refs):
    s_ref, mloc_ref, m_ref, l_ref, acc_ref = refs
    rows = n * CHUNK

    def scores(h):
        keys = k_ref[0, 0, h, pl.ds(j0, n)].reshape(rows, LANES)
        s = mask(lax.dot_general(keys, q_ref[0, 0, h, 0], _NT, preferred_element_type=jnp.float32))
        s_ref[h, :rows] = s
        mloc_ref[h] = jnp.max(s, axis=0, keepdims=True)

    def update(h):
        m = m_ref[h]
        m_new = jnp.maximum(m, mloc_ref[h])
        alpha = jnp.exp(m - m_new)
        p = jnp.exp(s_ref[h, :rows] - m_new)
        m_ref[h] = m_new
        l_ref[h] = alpha * l_ref[h] + jnp.sum(p, axis=0, keepdims=True)
        pb = p.astype(jnp.bfloat16)
        pv = jnp.dot(vt_ref[0, h, j0], pb[:CHUNK], preferred_element_type=jnp.float32)
        for c in range(1, n):
            pv = pv + jnp.dot(vt_ref[0, h, j0 + c], pb[c * CHUNK:(c + 1) * CHUNK], preferred_element_type=jnp.float32)
        acc_ref[h] = alpha * acc_ref[h] + pv

    _staged([scores, update])


def _split_head_pairs(q_ref, qz_ref, n_heads):
    tq = q_ref.shape[4]
    lane = lax.broadcasted_iota(jnp.int32, (tq, LANES), 1)
    for p in range(n_heads // 2):
        qp = q_ref[0, 0, p, 0]
        qz_ref[2 * p] = jnp.where(lane < HEAD_DIM, qp, jnp.zeros_like(qp))
        qz_ref[2 * p + 1] = jnp.where(lane >= HEAD_DIM, qp, jnp.zeros_like(qp))


def _store_heads(o_ref, head_out):
    for p in range(N_HEADS // 2):
        pair = jnp.concatenate([head_out(2 * p), head_out(2 * p + 1)], axis=0)
        o_ref[0, :, p * LANES:(p + 1) * LANES] = pair.T.astype(o_ref.dtype)


def _softmax_init(m_ref, l_ref, acc_ref):
    m_ref[...] = jnp.full(m_ref.shape, NEG_BIG, jnp.float32)
    l_ref[...] = jnp.zeros(l_ref.shape, jnp.float32)
    acc_ref[...] = jnp.zeros(acc_ref.shape, jnp.float32)


TILE_CHUNKS = 4


def _softmax_scratch(tq):
    vec = pltpu.VMEM((N_HEADS, 1, tq), jnp.float32)
    return [pltpu.VMEM((N_HEADS, TILE_CHUNKS * CHUNK, tq), jnp.float32), vec, vec, vec,
            pltpu.VMEM((N_HEADS, HEAD_DIM, tq), jnp.float32)]


def _softmax_chunks(n_chunks, tile):
    n_tiles = n_chunks // TILE_CHUNKS

    def body(t, _):
        tile(t * TILE_CHUNKS, TILE_CHUNKS)
        return 0

    lax.fori_loop(0, n_tiles, body, 0)
    done = n_tiles * TILE_CHUNKS
    size = TILE_CHUNKS // 2
    while size >= 1:
        has = ((n_chunks - done) & size) != 0

        @pl.when(has)
        def _(done=done, size=size):
            tile(done, size)

        done = done + jnp.where(has, size, 0)
        size //= 2


def _dsa_kernel(q_ref, k_ref, vt_ref, qi_ref, ki_ref, wt_ref, o_ref, keys_ref, hi_ref, lo_ref, qz_ref,
                s_ref, mloc_ref, m_ref, l_ref, acc_ref, *, topk):
    i = pl.program_id(1)
    tq = q_ref.shape[4]
    _split_head_pairs(qi_ref, qz_ref, IDX_HEADS)
    rel = (lax.broadcasted_iota(jnp.int32, (CHUNK, tq), 0)
           - lax.broadcasted_iota(jnp.int32, (CHUNK, tq), 1))

    def score_chunk(j, _):
        kc = ki_ref[0, j]
        sc = jnp.zeros((CHUNK, tq), jnp.float32)
        for h in range(IDX_HEADS):
            d = lax.dot_general(kc, qz_ref[h], _NT, preferred_element_type=jnp.float32)
            sc = sc + jnp.maximum(d, 0.0) * wt_ref[0, h:h + 1, :]
        bits = pltpu.bitcast(sc, jnp.int32)
        key = bits ^ ((bits >> 31) & jnp.int32(0x7FFFFFFF))
        key = jnp.where(rel <= (i - j) * CHUNK, key, jnp.int32(INT_MIN))
        keys_ref[j] = key
        hi_ref[j] = (key >> 16).astype(jnp.int16)
        lo_ref[j] = ((key & jnp.int32(0xFFFF)) + jnp.int32(I16_MIN)).astype(jnp.int16)
        return 0

    lax.fori_loop(0, i + 1, score_chunk, 0)

    rows16 = 16 * COUNT_ACCS

    def kth_largest16(ref, want):
        def count_ge(cand):
            cand16 = jnp.broadcast_to(cand, (rows16, tq)).astype(jnp.int16)

            def body(j, acc):
                for g in range(CHUNK // rows16):
                    ge = ref[j, g * rows16:(g + 1) * rows16, :] >= cand16
                    acc = acc + jnp.where(ge, jnp.int16(1), jnp.int16(0))
                return acc

            acc = lax.fori_loop(0, i + 1, body, jnp.zeros((rows16, tq), jnp.int16))
            return jnp.sum(acc.astype(jnp.float32), axis=0, keepdims=True)

        c0 = count_ge(jnp.zeros((1, tq), jnp.int32))
        ok = c0 >= want
        state = (jnp.where(ok, jnp.int32(0), jnp.int32(I16_MIN)), jnp.where(ok, 0.0, c0))

        def bit_body(n, state):
            t, above = state
            cand = t + jnp.left_shift(jnp.int32(1), 14 - n)
            c = count_ge(cand)
            ok = c >= want
            return jnp.where(ok, cand, t), jnp.where(ok, above, c)

        return lax.fori_loop(0, 15, bit_body, state)

    t_hi, above = kth_largest16(hi_ref, topk)
    t_hi16 = jnp.broadcast_to(t_hi, (16, tq)).astype(jnp.int16)

    def bucket(j, _):
        same = hi_ref[j].reshape(CHUNK // 16, 16, tq) == t_hi16
        lo = jnp.where(same, lo_ref[j].reshape(CHUNK // 16, 16, tq), jnp.int16(I16_MIN))
        lo_ref[j] = lo.reshape(CHUNK, tq)
        return 0

    lax.fori_loop(0, i + 1, bucket, 0)
    t_lo, _ = kth_largest16(lo_ref, topk - above)
    thr = (t_hi << 16) | (t_lo - jnp.int32(I16_MIN))
    thr = jnp.maximum(thr, jnp.int32(INT_MIN + 1))

    _softmax_init(m_ref, l_ref, acc_ref)

    def tile(j0, n):
        selected = keys_ref[pl.ds(j0, n)].reshape(n * CHUNK, tq) >= thr
        _softmax_tile(j0, n, q_ref, k_ref, vt_ref, lambda s: jnp.where(selected, s, NEG_BIG),
                      (s_ref, mloc_ref, m_ref, l_ref, acc_ref))

    _softmax_chunks(i + 1, tile)
    _store_heads(o_ref, lambda h: acc_ref[h] / l_ref[h])


def _dsa(pq, vt, ki, wt, topk):
    _, b, _, nc, _, _ = pq.shape
    tq = CHUNK
    kern = functools.partial(_dsa_kernel, topk=float(topk))
    return pl.pallas_call(
        kern,
        grid=(b, nc),
        in_specs=[pl.BlockSpec((1, 1, N_HEADS, 1, tq, LANES), lambda bi, i: (G_DSA_Q, bi, 0, i, 0, 0)),
                  pl.BlockSpec((1, 1, N_HEADS, nc, CHUNK, LANES), lambda bi, i: (G_DSA_K, bi, 0, 0, 0, 0)),
                  pl.BlockSpec((1, N_HEADS, nc, HEAD_DIM, CHUNK), lambda bi, i: (bi, 0, 0, 0, 0)),
                  pl.BlockSpec((1, 1, N_HEADS, 1, tq, LANES), lambda bi, i: (G_QI, bi, 0, i, 0, 0)),
                  pl.BlockSpec((1, nc, CHUNK, LANES), lambda bi, i: (bi, 0, 0, 0)),
                  pl.BlockSpec((1, IDX_HEADS, tq), lambda bi, i: (bi, 0, i))],
        out_specs=pl.BlockSpec((1, tq, BRANCH), lambda bi, i: (bi, i, 0)),
        out_shape=jax.ShapeDtypeStruct((b, nc * CHUNK, BRANCH), jnp.bfloat16),
        scratch_shapes=[pltpu.VMEM((nc, CHUNK, tq), jnp.int32),
                        pltpu.VMEM((nc, CHUNK, tq), jnp.int16), pltpu.VMEM((nc, CHUNK, tq), jnp.int16),
                        pltpu.VMEM((IDX_HEADS, tq, LANES), jnp.bfloat16)] + _softmax_scratch(tq),
        compiler_params=_params(("arbitrary", "arbitrary"), 56),
        name="dsa",
    )(pq, pq, vt, pq, ki, wt)


def _fox_kernel(q_ref, k_ref, vt_ref, o_ref, s_ref, mloc_ref, m_ref, l_ref, acc_ref):
    i = pl.program_id(1)
    tq = q_ref.shape[4]
    causal = (lax.broadcasted_iota(jnp.int32, (CHUNK, tq), 0)
              <= lax.broadcasted_iota(jnp.int32, (CHUNK, tq), 1))
    _softmax_init(m_ref, l_ref, acc_ref)
    refs = (s_ref, mloc_ref, m_ref, l_ref, acc_ref)
    _softmax_chunks(i, lambda j0, n: _softmax_tile(j0, n, q_ref, k_ref, vt_ref, lambda s: s, refs))
    _softmax_tile(i, 1, q_ref, k_ref, vt_ref, lambda s: jnp.where(causal, s, NEG_BIG), refs)
    _store_heads(o_ref, lambda h: acc_ref[h] / l_ref[h])


def _sb_scratch(tq):
    tile = pltpu.VMEM((N_HEADS, CHUNK, tq), jnp.bfloat16)
    vec = pltpu.VMEM((N_HEADS, 1, tq), jnp.float32)
    return [pltpu.VMEM((N_HEADS, tq, LANES), jnp.bfloat16),
            pltpu.VMEM((N_HEADS, CHUNK, tq), jnp.float32), tile, tile, tile, vec, vec,
            pltpu.VMEM((N_HEADS, HEAD_DIM, tq), jnp.float32)]


def _sb_kernel(q_ref, k_ref, vt_ref, o_ref, qz_ref, lb_ref, hi_ref, lo_ref, a_ref, later_ref, lom0_ref, acc_ref):
    i = pl.program_id(1)
    tq = q_ref.shape[4]
    strict = (lax.broadcasted_iota(jnp.int32, (CHUNK, tq), 0)
              < lax.broadcasted_iota(jnp.int32, (CHUNK, tq), 1))
    upper = jnp.where(lax.broadcasted_iota(jnp.int32, (CHUNK, CHUNK), 1)
                      > lax.broadcasted_iota(jnp.int32, (CHUNK, CHUNK), 0), 1.0, 0.0).astype(jnp.bfloat16)
    later_ref[...] = jnp.zeros(later_ref.shape, jnp.float32)
    acc_ref[...] = jnp.zeros(acc_ref.shape, jnp.float32)
    _split_head_pairs(q_ref, qz_ref, N_HEADS)

    def chunk(j, diag):
        def logits(h):
            z = lax.dot_general(k_ref[0, 0, h // 2, j], qz_ref[h], _NT, preferred_element_type=jnp.float32)
            log_beta = _log_sigmoid(z)
            lom = log_beta - z
            if diag:
                lom = jnp.where(strict, lom, 0.0)
            hi = pltpu.bitcast(pltpu.bitcast(lom, jnp.int32) & jnp.int32(-65536), jnp.float32)
            lb_ref[h] = log_beta
            hi_ref[h] = hi.astype(jnp.bfloat16)
            lo_ref[h] = (lom - hi).astype(jnp.bfloat16)
            lom0_ref[h] = lom[0:1, :]

        def weights(h):
            within = (jnp.dot(upper, hi_ref[h], preferred_element_type=jnp.float32)
                      + jnp.dot(upper, lo_ref[h], preferred_element_type=jnp.float32))
            a = jnp.exp(lb_ref[h] + within + later_ref[h])
            if diag:
                a = jnp.where(strict, a, 0.0)
            a_ref[h] = a.astype(jnp.bfloat16)
            later_ref[h] += within[0:1, :] + lom0_ref[h]

        def values(h):
            acc_ref[h] += jnp.dot(vt_ref[0, h, j], a_ref[h], preferred_element_type=jnp.float32)

        _staged([logits, weights, values])
        return 0

    chunk(i, True)

    def live(state):
        n, later_max = state
        return jnp.logical_and(n < i, later_max > EXP_IS_ZERO_BELOW)

    def step(state):
        n, _ = state
        chunk(i - 1 - n, False)
        return n + 1, jnp.max(later_ref[...])

    lax.while_loop(live, step, (jnp.int32(0), jnp.max(later_ref[...])))
    _store_heads(o_ref, lambda h: acc_ref[h])


def _causal_attention(kernel, name, scratch, pq, vt, q_at, k_at, branch):
    _, b, _, nc, _, _ = pq.shape
    tq = CHUNK
    (g_q, q_blk, q_n), (g_k, k_blk, k_n) = q_at, k_at
    return pl.pallas_call(
        kernel,
        grid=(b, nc),
        in_specs=[pl.BlockSpec((1, 1, q_n, 1, tq, LANES), lambda bi, i: (g_q, bi, q_blk, i, 0, 0)),
                  pl.BlockSpec((1, 1, k_n, nc, CHUNK, LANES), lambda bi, i: (g_k, bi, k_blk, 0, 0, 0)),
                  pl.BlockSpec((1, N_HEADS, nc, HEAD_DIM, CHUNK), lambda bi, i: (bi, branch, 0, 0, 0))],
        out_specs=pl.BlockSpec((1, tq, BRANCH), lambda bi, i: (bi, i, 0)),
        out_shape=jax.ShapeDtypeStruct((b, nc * CHUNK, BRANCH), jnp.bfloat16),
        scratch_shapes=scratch(tq),
        compiler_params=_params(("arbitrary", "arbitrary"), 48),
        name=name,
    )(pq, pq, vt)


def _merge_kernel(oa_ref, ob_ref, oc_ref, g_ref, x_ref, wb_ref, wo_ref, gam_ref, bet_ref, o_ref, *, alpha):
    ups = [jnp.dot(o[0], wb_ref[n], preferred_element_type=jnp.float32)
           for n, o in enumerate((oa_ref, ob_ref, oc_ref))]
    cols = []
    for c in range(N_HEADS):
        sl = slice(c * LANES, (c + 1) * LANES)
        mc = ups[0][:, sl] * g_ref[0, 0, c].astype(jnp.float32)
        for n in range(1, N_BRANCHES):
            mc = mc + ups[n][:, sl] * g_ref[n, 0, c].astype(jnp.float32)
        cols.append(mc.astype(jnp.bfloat16))
    merged = jnp.concatenate(cols, axis=1)
    y = jnp.dot(merged, wo_ref[...], preferred_element_type=jnp.float32)
    o_ref[0] = _layer_norm(alpha * x_ref[0] + y, gam_ref[...], bet_ref[...])


def _merge(oa, ob, oc, pq5, x, wb, wo, gam, bet, alpha, t=512):
    b, s, d = x.shape
    o_spec = pl.BlockSpec((1, t, BRANCH), lambda bi, si: (bi, si, 0))
    return pl.pallas_call(
        functools.partial(_merge_kernel, alpha=alpha),
        grid=(b, s // t),
        in_specs=[o_spec, o_spec, o_spec,
                  pl.BlockSpec((N_BRANCHES, 1, N_HEADS, t, LANES), lambda bi, si: (G_GATES // N_BRANCHES, bi, 0, si, 0)),
                  pl.BlockSpec((1, t, d), lambda bi, si: (bi, si, 0)),
                  pl.BlockSpec((N_BRANCHES, BRANCH, d), lambda bi, si: (0, 0, 0)),
                  pl.BlockSpec((d, d), lambda bi, si: (0, 0)),
                  pl.BlockSpec((1, d), lambda bi, si: (0, 0)),
                  pl.BlockSpec((1, d), lambda bi, si: (0, 0))],
        out_specs=pl.BlockSpec((1, t, d), lambda bi, si: (bi, si, 0)),
        out_shape=jax.ShapeDtypeStruct((b, s, d), jnp.float32),
        compiler_params=_params(("arbitrary", "arbitrary"), 48),
        name="merge",
    )(oa, ob, oc, pq5, x, wb, wo, gam, bet)


def _ffn_kernel(x_ref, w1_ref, w2_ref, gam_ref, bet_ref, o_ref, *, alpha, fchunk):
    x = x_ref[0]
    xb = x.astype(jnp.bfloat16)
    y = jnp.zeros_like(x)
    for c in range(w1_ref.shape[1] // fchunk):
        sl = slice(c * fchunk, (c + 1) * fchunk)
        h = jnp.dot(xb, w1_ref[:, sl], preferred_element_type=jnp.float32)
        h = jnp.square(jnp.maximum(h, 0.0)).astype(jnp.bfloat16)
        y = y + jnp.dot(h, w2_ref[sl, :], preferred_element_type=jnp.float32)
    o_ref[0] = _layer_norm(alpha * x + y, gam_ref[...], bet_ref[...])


def _ffn(x, w1, w2, gam, bet, alpha, t=512):
    b, s, d = x.shape
    f = w1.shape[1]
    return pl.pallas_call(
        functools.partial(_ffn_kernel, alpha=alpha, fchunk=1024),
        grid=(b, s // t),
        in_specs=[pl.BlockSpec((1, t, d), lambda bi, si: (bi, si, 0)),
                  pl.BlockSpec((d, f), lambda bi, si: (0, 0)),
                  pl.BlockSpec((f, d), lambda bi, si: (0, 0)),
                  pl.BlockSpec((1, d), lambda bi, si: (0, 0)),
                  pl.BlockSpec((1, d), lambda bi, si: (0, 0))],
        out_specs=pl.BlockSpec((1, t, d), lambda bi, si: (bi, si, 0)),
        out_shape=jax.ShapeDtypeStruct((b, s, d), jnp.float32),
        compiler_params=_params(("arbitrary", "arbitrary"), 56),
        name="ffn",
    )(x, w1, w2, gam, bet)


def _pad_heads(w, n_heads, scale=1.0):
    d = w.shape[0]
    w = (w * scale).reshape(d, n_heads, HEAD_DIM)
    return jnp.pad(w, ((0, 0), (0, 0), (0, LANES - HEAD_DIM))).reshape(d, n_heads * LANES)


def _aug_weights(slopes):
    wa = np.zeros((N_BIAS_GROUPS, LANES, N_HEADS * LANES), np.float32)
    for h in range(N_HEADS):
        base = h * LANES + AUG_LANE
        sl = float(slopes[h])
        wa[G_DSA_Q, XE_A, base + 0] = -sl * CHUNK
        wa[G_DSA_Q, XE_B, base + 1] = -sl
        wa[G_DSA_Q, XE_ONE, base + 2] = 1.0
        wa[G_DSA_Q, XE_ONE, base + 3] = 1.0
        wa[G_DSA_K, XE_ONE, base + 0] = 1.0
        wa[G_DSA_K, XE_ONE, base + 1] = 1.0
        wa[G_DSA_K, XE_A, base + 2] = sl * CHUNK
        wa[G_DSA_K, XE_B, base + 3] = sl
        for p in range(3):
            wa[G_FOX_Q, XE_ONE, base + p] = 1.0
            wa[G_FOX_K, XE_C + N_HEADS * p + h, base + p] = -1.0
    return wa


def _layer_weights(w_in, b_forget, slopes):
    d = w_in.shape[0]
    sizes = (BRANCH, BRANCH, BRANCH, IDX_HEADS * IDX_DIM, IDX_DIM, IDX_HEADS,
             BRANCH, BRANCH, BRANCH, N_HEADS, BRANCH, BRANCH, BRANCH, N_BRANCHES * d)
    offs = np.concatenate([[0], np.cumsum(sizes)])
    (qa, ka, va, qi, ki, wi, qf, kf, vf, fg, qs, ks, vs, gates) = [w_in[:, offs[n]:offs[n + 1]] for n in range(len(sizes))]
    scale = HEAD_DIM ** -0.5
    groups = [_pad_heads(qa, N_HEADS, scale), _pad_heads(ka, N_HEADS),
              _pad_heads(qf, N_HEADS, scale), _pad_heads(kf, N_HEADS),
              jnp.concatenate([qs * scale, ks], axis=1), qi]
    groups += [gates[:, n * d:(n + 1) * d] for n in range(N_BRANCHES)]
    w_main = jnp.stack(groups).astype(jnp.bfloat16)
    w_small = jnp.concatenate([fg, wi, jnp.zeros((d, LANES - N_HEADS - IDX_HEADS), w_in.dtype), ki, ki], axis=1)
    b_small = jnp.pad(b_forget, (0, LANES - N_HEADS)).reshape(1, LANES)
    w_vt = jnp.concatenate([va, vf, vs], axis=1).T.astype(jnp.bfloat16)
    return w_main, w_small.astype(jnp.bfloat16), b_small, w_vt


def kernel(x, w_in, b_forget, w_branch, w_out, ln1_g, ln1_b, w_ff1, w_ff2, ln2_g, ln2_b):
    depth = w_in.shape[0]
    b, s, d = x.shape
    assert s % CHUNK == 0 and d == N_HEADS * LANES
    alpha = (2.0 * depth) ** 0.25
    topk = min(TOPK_MAX, s // 4)
    slopes = 2.0 ** (-8.0 * np.arange(1, N_HEADS + 1, dtype=np.float64) / N_HEADS)
    w_aug_np = _aug_weights(slopes)
    assert np.array_equal(w_aug_np, w_aug_np.astype(jnp.bfloat16).astype(np.float32))
    w_aug = jnp.asarray(w_aug_np, jnp.bfloat16)
    nc = s // CHUNK

    for layer in range(depth):
        w_main, w_small, b_small, w_vt = _layer_weights(w_in[layer], b_forget[layer], slopes)
        xb, xe, small, ki = _prep(x, w_small, b_small)
        pq5 = _proj(xb, xe, w_main, w_aug)
        pq = pq5.reshape(N_GROUPS, b, N_HEADS, nc, CHUNK, LANES)
        vt = _proj_vt(xb, w_vt)
        wt = small[:, :, N_HEADS:N_HEADS + IDX_HEADS].transpose(0, 2, 1)
        o_a = _dsa(pq, vt, ki.reshape(b, nc, CHUNK, LANES), wt, topk)
        o_b = _causal_attention(_fox_kernel, "fox", _softmax_scratch, pq, vt,
                                (G_FOX_Q, 0, N_HEADS), (G_FOX_K, 0, N_HEADS), 1)
        o_c = _causal_attention(_sb_kernel, "sb", _sb_scratch, pq, vt,
                                (G_SB, 0, N_HEADS // 2), (G_SB, 1, N_HEADS // 2), 2)
        x = _merge(o_a, o_b, o_c, pq5, x, w_branch[layer].astype(jnp.bfloat16), w_out[layer].astype(jnp.bfloat16),
                   ln1_g[layer].reshape(1, d), ln1_b[layer].reshape(1, d), alpha)
        x = _ffn(x, w_ff1[layer].astype(jnp.bfloat16), w_ff2[layer].astype(jnp.bfloat16),
                 ln2_g[layer].reshape(1, d), ln2_b[layer].reshape(1, d), alpha)
    return x
```

```python
import functools

import numpy as np
import jax
import jax.numpy as jnp
from jax import lax
from jax.experimental import pallas as pl
from jax.experimental.pallas import tpu as pltpu

HEAD_DIM = 64
N_HEADS = 8
BRANCH = N_HEADS * HEAD_DIM
IDX_HEADS = 16
IDX_DIM = 64
TOPK_MAX = 256
N_BRANCHES = 3
LN_EPS = 1e-5

LANES = 128
CHUNK = 256
N_GROUPS = 9
G_DSA_Q, G_DSA_K, G_FOX_Q, G_FOX_K, G_SB, G_QI, G_GATES = 0, 1, 2, 3, 4, 5, 6
N_BIAS_GROUPS = 4
PROJ_BLOCK_LANES = 2
AUG_LANE = HEAD_DIM
XE_A, XE_B, XE_ONE, XE_C = 0, 1, 2, 3
NEG_BIG = -1e30
EXP_IS_ZERO_BELOW = -105.0
INT_MIN = -(2 ** 31)
I16_MIN = -(2 ** 15)

_NT = (((1,), (1,)), ((), ()))


def _log_sigmoid(x):
    return jnp.minimum(x, 0.0) - jnp.log(1.0 + jnp.exp(-jnp.abs(x)))


def _split3(x):
    a = x.astype(jnp.bfloat16)
    r = x - a.astype(jnp.float32)
    b = r.astype(jnp.bfloat16)
    c = (r - b.astype(jnp.float32)).astype(jnp.bfloat16)
    return a, b, c


def _layer_norm(z, g, b):
    mu = jnp.mean(z, axis=-1, keepdims=True)
    zc = z - mu
    var = jnp.mean(zc * zc, axis=-1, keepdims=True)
    return zc * lax.rsqrt(var + LN_EPS) * g + b


def _params(sem, vmem_mb):
    return pltpu.CompilerParams(dimension_semantics=sem, vmem_limit_bytes=vmem_mb * 1024 * 1024)


def _prep_kernel(x_ref, ws_ref, bs_ref, xb_ref, xe_ref, sm_ref, ki_ref, carry_ref):
    s_blk = pl.program_id(1)
    t = x_ref.shape[1]
    xb = x_ref[0].astype(jnp.bfloat16)
    xb_ref[0] = xb
    both = jnp.dot(xb, ws_ref[0], preferred_element_type=jnp.float32)
    sm = both[:, :LANES]
    sm_ref[0] = sm
    ki_ref[0] = both[:, LANES:].astype(jnp.bfloat16)
    fg = sm + bs_ref[...]
    lf = jnp.minimum(fg, 0.0) - jnp.log1p(jnp.exp(-jnp.abs(fg)))

    @pl.when(s_blk == 0)
    def _():
        carry_ref[...] = jnp.zeros_like(carry_ref)

    row = lax.broadcasted_iota(jnp.int32, (t, t), 0)
    col = lax.broadcasted_iota(jnp.int32, (t, t), 1)
    tri = jnp.where(col <= row, 1.0, 0.0).astype(jnp.bfloat16)
    cs = carry_ref[0:1, :]
    for part in _split3(lf):
        cs = cs + jnp.dot(tri, part, preferred_element_type=jnp.float32)
    carry_ref[...] = jnp.broadcast_to(cs[t - 1:t, :], carry_ref.shape)

    er = lax.broadcasted_iota(jnp.int32, (LANES, LANES), 0)
    ec = lax.broadcasted_iota(jnp.int32, (LANES, LANES), 1)
    xe = jnp.zeros((t, LANES), jnp.float32)
    for p, part in enumerate(_split3(cs)):
        place = jnp.where(ec == er + (XE_C + N_HEADS * p), jnp.where(er < N_HEADS, 1.0, 0.0), 0.0)
        xe = xe + jnp.dot(part, place.astype(jnp.bfloat16), preferred_element_type=jnp.float32)
    pos = s_blk * t + lax.broadcasted_iota(jnp.int32, (t, LANES), 0)
    lane = lax.broadcasted_iota(jnp.int32, (t, LANES), 1)
    xe = jnp.where(lane == XE_A, jnp.right_shift(pos, 8).astype(jnp.float32), xe)
    xe = jnp.where(lane == XE_B, (pos & (CHUNK - 1)).astype(jnp.float32), xe)
    xe = jnp.where(lane == XE_ONE, 1.0, xe)
    xe_ref[0] = xe.astype(jnp.bfloat16)


def _prep(x, w_small, layer, b_small, t=512):
    b, s, d = x.shape
    return pl.pallas_call(
        _prep_kernel,
        grid=(b, s // t),
        in_specs=[pl.BlockSpec((1, t, d), lambda bi, si: (bi, si, 0)),
                  pl.BlockSpec((1, d, 2 * LANES), lambda bi, si: (layer, 0, 0)),
                  pl.BlockSpec((1, LANES), lambda bi, si: (0, 0))],
        out_specs=[pl.BlockSpec((1, t, d), lambda bi, si: (bi, si, 0)),
                   pl.BlockSpec((1, t, LANES), lambda bi, si: (bi, si, 0)),
                   pl.BlockSpec((1, t, LANES), lambda bi, si: (bi, si, 0)),
                   pl.BlockSpec((1, t, LANES), lambda bi, si: (bi, si, 0))],
        out_shape=[jax.ShapeDtypeStruct((b, s, d), jnp.bfloat16),
                   jax.ShapeDtypeStruct((b, s, LANES), jnp.bfloat16),
                   jax.ShapeDtypeStruct((b, s, LANES), jnp.float32),
                   jax.ShapeDtypeStruct((b, s, LANES), jnp.bfloat16)],
        scratch_shapes=[pltpu.VMEM((8, LANES), jnp.float32)],
        compiler_params=_params(("arbitrary", "arbitrary"), 32),
        name="prep",
    )(x, w_small, b_small)


def _proj_kernel(xb_ref, xe_ref, w_ref, wa_ref, o_ref):
    j = pl.program_id(0)

    def column_blocks(with_bias):
        is_gate = jnp.logical_and(j >= G_GATES, j < G_GATES + N_BRANCHES)
        width = PROJ_BLOCK_LANES * LANES
        for c in range(N_HEADS // PROJ_BLOCK_LANES):
            sl = slice(c * width, (c + 1) * width)
            acc = jnp.dot(xb_ref[0], w_ref[0, 0, :, sl], preferred_element_type=jnp.float32)
            if with_bias:
                acc = acc + jnp.dot(xe_ref[0], wa_ref[0, :, sl], preferred_element_type=jnp.float32)
            else:
                acc = jnp.where(is_gate, jax.nn.sigmoid(acc), acc)
            for g in range(PROJ_BLOCK_LANES):
                o_ref[0, 0, PROJ_BLOCK_LANES * c + g] = acc[:, g * LANES:(g + 1) * LANES].astype(jnp.bfloat16)

    @pl.when(j < N_BIAS_GROUPS)
    def _():
        column_blocks(True)

    @pl.when(j >= N_BIAS_GROUPS)
    def _():
        column_blocks(False)


def _proj(xb, xe, w_main, layer, w_aug, t=1024):
    b, s, d = xb.shape
    assert s % t == 0
    ns = s // t
    return pl.pallas_call(
        _proj_kernel,
        grid=(N_GROUPS, b * ns),
        in_specs=[pl.BlockSpec((1, t, d), lambda j, i: (i // ns, i % ns, 0)),
                  pl.BlockSpec((1, t, LANES), lambda j, i: (i // ns, i % ns, 0)),
                  pl.BlockSpec((1, 1, d, N_HEADS * LANES), lambda j, i: (layer, j, 0, 0)),
                  pl.BlockSpec((1, LANES, N_HEADS * LANES), lambda j, i: (jnp.minimum(j, N_BIAS_GROUPS - 1), 0, 0))],
        out_specs=pl.BlockSpec((1, 1, N_HEADS, t, LANES), lambda j, i: (j, i // ns, 0, i % ns, 0)),
        out_shape=jax.ShapeDtypeStruct((N_GROUPS, b, N_HEADS, s, LANES), jnp.bfloat16),
        compiler_params=_params(("arbitrary", "arbitrary"), 48),
        name="proj",
    )(xb, xe, w_main, w_aug)


def _proj_vt_kernel(xb_ref, w_ref, o_ref):
    vt = lax.dot_general(w_ref[0], xb_ref[0], _NT, preferred_element_type=jnp.float32)
    for c in range(xb_ref.shape[1] // CHUNK):
        blk = vt[:, c * CHUNK:(c + 1) * CHUNK].astype(jnp.bfloat16)
        o_ref[0, :, c] = blk.reshape(N_BRANCHES * N_HEADS, HEAD_DIM, CHUNK)


def _proj_vt(xb, w_vt, layer, t=1024):
    b, s, d = xb.shape
    nh = N_BRANCHES * N_HEADS
    return pl.pallas_call(
        _proj_vt_kernel,
        grid=(b, s // t),
        in_specs=[pl.BlockSpec((1, t, d), lambda bi, si: (bi, si, 0)),
                  pl.BlockSpec((1, nh * HEAD_DIM, d), lambda bi, si: (layer, 0, 0))],
        out_specs=pl.BlockSpec((1, nh, t // CHUNK, HEAD_DIM, CHUNK), lambda bi, si: (bi, 0, si, 0, 0)),
        out_shape=jax.ShapeDtypeStruct((b, nh, s // CHUNK, HEAD_DIM, CHUNK), jnp.bfloat16),
        compiler_params=_params(("arbitrary", "arbitrary"), 48),
        name="proj_vt",
    )(xb, w_vt)


STAGE_LAG = 8
COUNT_ACCS = 4


def _staged(stages):
    for step in range(N_HEADS + STAGE_LAG * (len(stages) - 1)):
        for n, stage in enumerate(stages):
            if 0 <= step - STAGE_LAG * n < N_HEADS:
                stage(step - STAGE_LAG * n)


def _softmax_tile(j0, n, q_ref, k_ref, vt_ref, mask, refs):
    s_ref, mloc_ref, m_ref, l_ref, acc_ref = refs
    rows = n * CHUNK

    def scores(h):
        keys = k_ref[0, 0, h, pl.ds(j0, n)].reshape(rows, LANES)
        s = mask(lax.dot_general(keys, q_ref[0, 0, h, 0], _NT, preferred_element_type=jnp.float32))
        s_ref[h, :rows] = s
        mloc_ref[h] = jnp.max(s, axis=0, keepdims=True)

    def update(h):
        m = m_ref[h]
        m_new = jnp.maximum(m, mloc_ref[h])
        alpha = jnp.exp(m - m_new)
        p = jnp.exp(s_ref[h, :rows] - m_new)
        m_ref[h] = m_new
        l_ref[h] = alpha * l_ref[h] + jnp.sum(p, axis=0, keepdims=True)
        pb = p.astype(jnp.bfloat16)
        pv = jnp.dot(vt_ref[0, h, j0], pb[:CHUNK], preferred_element_type=jnp.float32)
        for c in range(1, n):
            pv = pv + jnp.dot(vt_ref[0, h, j0 + c], pb[c * CHUNK:(c + 1) * CHUNK], preferred_element_type=jnp.float32)
        acc_ref[h] = alpha * acc_ref[h] + pv

    _staged([scores, update])


def _split_head_pairs(q_ref, qz_ref, n_heads):
    tq = q_ref.shape[4]
    lane = lax.broadcasted_iota(jnp.int32, (tq, LANES), 1)
    for p in range(n_heads // 2):
        qp = q_ref[0, 0, p, 0]
        qz_ref[2 * p] = jnp.where(lane < HEAD_DIM, qp, jnp.zeros_like(qp))
        qz_ref[2 * p + 1] = jnp.where(lane >= HEAD_DIM, qp, jnp.zeros_like(qp))


def _store_heads(o_ref, head_out):
    for p in range(N_HEADS // 2):
        pair = jnp.concatenate([head_out(2 * p), head_out(2 * p + 1)], axis=0)
        o_ref[0, :, p * LANES:(p + 1) * LANES] = pair.T.astype(o_ref.dtype)


def _softmax_init(m_ref, l_ref, acc_ref):
    m_ref[...] = jnp.full(m_ref.shape, NEG_BIG, jnp.float32)
    l_ref[...] = jnp.zeros(l_ref.shape, jnp.float32)
    acc_ref[...] = jnp.zeros(acc_ref.shape, jnp.float32)


TILE_CHUNKS = 4


def _softmax_scratch(tq):
    vec = pltpu.VMEM((N_HEADS, 1, tq), jnp.float32)
    return [pltpu.VMEM((N_HEADS, TILE_CHUNKS * CHUNK, tq), jnp.float32), vec, vec, vec,
            pltpu.VMEM((N_HEADS, HEAD_DIM, tq), jnp.float32)]


def _softmax_chunks(n_chunks, tile):
    n_tiles = n_chunks // TILE_CHUNKS

    def body(t, _):
        tile(t * TILE_CHUNKS, TILE_CHUNKS)
        return 0

    lax.fori_loop(0, n_tiles, body, 0)
    done = n_tiles * TILE_CHUNKS
    size = TILE_CHUNKS // 2
    while size >= 1:
        has = ((n_chunks - done) & size) != 0

        @pl.when(has)
        def _(done=done, size=size):
            tile(done, size)

        done = done + jnp.where(has, size, 0)
        size //= 2


def _dsa_kernel(q_ref, k_ref, vt_ref, qi_ref, ki_ref, wt_ref, o_ref, keys_ref, hi_ref, lo_ref, qz_ref,
                s_ref, mloc_ref, m_ref, l_ref, acc_ref, *, topk):
    i = pl.program_id(1)
    tq = q_ref.shape[4]
    _split_head_pairs(qi_ref, qz_ref, IDX_HEADS)
    rel = (lax.broadcasted_iota(jnp.int32, (CHUNK, tq), 0)
           - lax.broadcasted_iota(jnp.int32, (CHUNK, tq), 1))

    def score_chunks(j0, n):
        kc = ki_ref[0, pl.ds(j0, n)].reshape(n * CHUNK, LANES)
        sc = jnp.zeros((n * CHUNK, tq), jnp.float32)
        for h in range(IDX_HEADS):
            d = lax.dot_general(kc, qz_ref[h], _NT, preferred_element_type=jnp.float32)
            sc = sc + jnp.maximum(d, 0.0) * wt_ref[0, h:h + 1, :]
        bits = pltpu.bitcast(sc, jnp.int32)
        keys = bits ^ ((bits >> 31) & jnp.int32(0x7FFFFFFF))
        for c in range(n):
            key = jnp.where(rel <= (i - j0 - c) * CHUNK, keys[c * CHUNK:(c + 1) * CHUNK], jnp.int32(INT_MIN))
            keys_ref[j0 + c] = key
            hi_ref[j0 + c] = (key >> 16).astype(jnp.int16)
            lo_ref[j0 + c] = ((key & jnp.int32(0xFFFF)) + jnp.int32(I16_MIN)).astype(jnp.int16)

    def score_pair(t, _):
        score_chunks(2 * t, 2)
        return 0

    lax.fori_loop(0, (i + 1) // 2, score_pair, 0)

    @pl.when((i + 1) % 2 == 1)
    def _():
        score_chunks(i, 1)

    rows16 = 16 * COUNT_ACCS

    def kth_largest16(ref, want):
        def count_ge(cand):
            cand16 = jnp.broadcast_to(cand, (rows16, tq)).astype(jnp.int16)

            def body(j, acc):
                for g in range(CHUNK // rows16):
                    ge = ref[j, g * rows16:(g + 1) * rows16, :] >= cand16
                    acc = acc + jnp.where(ge, jnp.int16(1), jnp.int16(0))
                return acc

            acc = lax.fori_loop(0, i + 1, body, jnp.zeros((rows16, tq), jnp.int16))
            return jnp.sum(acc.astype(jnp.float32), axis=0, keepdims=True)

        c0 = count_ge(jnp.zeros((1, tq), jnp.int32))
        ok = c0 >= want
        state = (jnp.where(ok, jnp.int32(0), jnp.int32(I16_MIN)), jnp.where(ok, 0.0, c0))

        def bit_body(n, state):
            t, above = state
            cand = t + jnp.left_shift(jnp.int32(1), 14 - n)
            c = count_ge(cand)
            ok = c >= want
            return jnp.where(ok, cand, t), jnp.where(ok, above, c)

        return lax.fori_loop(0, 15, bit_body, state)

    t_hi, above = kth_largest16(hi_ref, topk)
    t_hi16 = jnp.broadcast_to(t_hi, (16, tq)).astype(jnp.int16)

    def bucket(j, _):
        same = hi_ref[j].reshape(CHUNK // 16, 16, tq) == t_hi16
        lo = jnp.where(same, lo_ref[j].reshape(CHUNK // 16, 16, tq), jnp.int16(I16_MIN))
        lo_ref[j] = lo.reshape(CHUNK, tq)
        return 0

    lax.fori_loop(0, i + 1, bucket, 0)
    t_lo, _ = kth_largest16(lo_ref, topk - above)
    thr = (t_hi << 16) | (t_lo - jnp.int32(I16_MIN))
    thr = jnp.maximum(thr, jnp.int32(INT_MIN + 1))

    _softmax_init(m_ref, l_ref, acc_ref)

    def tile(j0, n):
        selected = keys_ref[pl.ds(j0, n)].reshape(n * CHUNK, tq) >= thr
        _softmax_tile(j0, n, q_ref, k_ref, vt_ref, lambda s: jnp.where(selected, s, NEG_BIG),
                      (s_ref, mloc_ref, m_ref, l_ref, acc_ref))

    _softmax_chunks(i + 1, tile)
    _store_heads(o_ref, lambda h: acc_ref[h] / l_ref[h])


def _dsa(pq, vt, ki, wt, topk):
    _, b, _, nc, _, _ = pq.shape
    tq = CHUNK
    kern = functools.partial(_dsa_kernel, topk=float(topk))
    return pl.pallas_call(
        kern,
        grid=(b, nc),
        in_specs=[pl.BlockSpec((1, 1, N_HEADS, 1, tq, LANES), lambda bi, i: (G_DSA_Q, bi, 0, i, 0, 0)),
                  pl.BlockSpec((1, 1, N_HEADS, nc, CHUNK, LANES), lambda bi, i: (G_DSA_K, bi, 0, 0, 0, 0)),
                  pl.BlockSpec((1, N_HEADS, nc, HEAD_DIM, CHUNK), lambda bi, i: (bi, 0, 0, 0, 0)),
                  pl.BlockSpec((1, 1, N_HEADS, 1, tq, LANES), lambda bi, i: (G_QI, bi, 0, i, 0, 0)),
                  pl.BlockSpec((1, nc, CHUNK, LANES), lambda bi, i: (bi, 0, 0, 0)),
                  pl.BlockSpec((1, IDX_HEADS, tq), lambda bi, i: (bi, 0, i))],
        out_specs=pl.BlockSpec((1, tq, BRANCH), lambda bi, i: (bi, i, 0)),
        out_shape=jax.ShapeDtypeStruct((b, nc * CHUNK, BRANCH), jnp.bfloat16),
        scratch_shapes=[pltpu.VMEM((nc, CHUNK, tq), jnp.int32),
                        pltpu.VMEM((nc, CHUNK, tq), jnp.int16), pltpu.VMEM((nc, CHUNK, tq), jnp.int16),
                        pltpu.VMEM((IDX_HEADS, tq, LANES), jnp.bfloat16)] + _softmax_scratch(tq),
        compiler_params=_params(("arbitrary", "arbitrary"), 56),
        name="dsa",
    )(pq, pq, vt, pq, ki, wt)


def _fox_kernel(q_ref, k_ref, vt_ref, o_ref, s_ref, mloc_ref, m_ref, l_ref, acc_ref):
    i = pl.program_id(1)
    tq = q_ref.shape[4]
    causal = (lax.broadcasted_iota(jnp.int32, (CHUNK, tq), 0)
              <= lax.broadcasted_iota(jnp.int32, (CHUNK, tq), 1))
    _softmax_init(m_ref, l_ref, acc_ref)
    refs = (s_ref, mloc_ref, m_ref, l_ref, acc_ref)
    _softmax_chunks(i, lambda j0, n: _softmax_tile(j0, n, q_ref, k_ref, vt_ref, lambda s: s, refs))
    _softmax_tile(i, 1, q_ref, k_ref, vt_ref, lambda s: jnp.where(causal, s, NEG_BIG), refs)
    _store_heads(o_ref, lambda h: acc_ref[h] / l_ref[h])


def _sb_scratch(tq):
    tile = pltpu.VMEM((N_HEADS, CHUNK, tq), jnp.bfloat16)
    vec = pltpu.VMEM((N_HEADS, 1, tq), jnp.float32)
    return [pltpu.VMEM((N_HEADS, tq, LANES), jnp.bfloat16),
            pltpu.VMEM((N_HEADS, CHUNK, tq), jnp.float32), tile, tile, tile, vec, vec,
            pltpu.VMEM((N_HEADS, HEAD_DIM, tq), jnp.float32)]


def _sb_kernel(q_ref, k_ref, vt_ref, o_ref, qz_ref, lb_ref, hi_ref, lo_ref, a_ref, later_ref, lom0_ref, acc_ref):
    i = pl.program_id(1)
    tq = q_ref.shape[4]
    strict = (lax.broadcasted_iota(jnp.int32, (CHUNK, tq), 0)
              < lax.broadcasted_iota(jnp.int32, (CHUNK, tq), 1))
    upper = jnp.where(lax.broadcasted_iota(jnp.int32, (CHUNK, CHUNK), 1)
                      > lax.broadcasted_iota(jnp.int32, (CHUNK, CHUNK), 0), 1.0, 0.0).astype(jnp.bfloat16)
    later_ref[...] = jnp.zeros(later_ref.shape, jnp.float32)
    acc_ref[...] = jnp.zeros(acc_ref.shape, jnp.float32)
    _split_head_pairs(q_ref, qz_ref, N_HEADS)

    def chunk(j, diag):
        def logits(h):
            z = lax.dot_general(k_ref[0, 0, h // 2, j], qz_ref[h], _NT, preferred_element_type=jnp.float32)
            log_beta = _log_sigmoid(z)
            lom = log_beta - z
            if diag:
                lom = jnp.where(strict, lom, 0.0)
            hi = pltpu.bitcast(pltpu.bitcast(lom, jnp.int32) & jnp.int32(-65536), jnp.float32)
            lb_ref[h] = log_beta
            hi_ref[h] = hi.astype(jnp.bfloat16)
            lo_ref[h] = (lom - hi).astype(jnp.bfloat16)
            lom0_ref[h] = lom[0:1, :]

        def weights(h):
            within = (jnp.dot(upper, hi_ref[h], preferred_element_type=jnp.float32)
                      + jnp.dot(upper, lo_ref[h], preferred_element_type=jnp.float32))
            a = jnp.exp(lb_ref[h] + within + later_ref[h])
            if diag:
                a = jnp.where(strict, a, 0.0)
            a_ref[h] = a.astype(jnp.bfloat16)
            later_ref[h] += within[0:1, :] + lom0_ref[h]

        def values(h):
            acc_ref[h] += jnp.dot(vt_ref[0, h, j], a_ref[h], preferred_element_type=jnp.float32)

        _staged([logits, weights, values])
        return 0

    chunk(i, True)

    def live(state):
        n, later_max = state
        return jnp.logical_and(n < i, later_max > EXP_IS_ZERO_BELOW)

    def step(state):
        n, _ = state
        chunk(i - 1 - n, False)
        return n + 1, jnp.max(later_ref[...])

    lax.while_loop(live, step, (jnp.int32(0), jnp.max(later_ref[...])))
    _store_heads(o_ref, lambda h: acc_ref[h])


def _causal_attention(kernel, name, scratch, pq, vt, q_at, k_at, branch):
    _, b, _, nc, _, _ = pq.shape
    tq = CHUNK
    (g_q, q_blk, q_n), (g_k, k_blk, k_n) = q_at, k_at
    return pl.pallas_call(
        kernel,
        grid=(b, nc),
        in_specs=[pl.BlockSpec((1, 1, q_n, 1, tq, LANES), lambda bi, i: (g_q, bi, q_blk, i, 0, 0)),
                  pl.BlockSpec((1, 1, k_n, nc, CHUNK, LANES), lambda bi, i: (g_k, bi, k_blk, 0, 0, 0)),
                  pl.BlockSpec((1, N_HEADS, nc, HEAD_DIM, CHUNK), lambda bi, i: (bi, branch, 0, 0, 0))],
        out_specs=pl.BlockSpec((1, tq, BRANCH), lambda bi, i: (bi, i, 0)),
        out_shape=jax.ShapeDtypeStruct((b, nc * CHUNK, BRANCH), jnp.bfloat16),
        scratch_shapes=scratch(tq),
        compiler_params=_params(("arbitrary", "arbitrary"), 48),
        name=name,
    )(pq, pq, vt)


def _merge_kernel(oa_ref, ob_ref, oc_ref, g_ref, x_ref, wb_ref, wo_ref, gam_ref, bet_ref, o_ref, *, alpha):
    ups = [jnp.dot(o[0], wb_ref[n], preferred_element_type=jnp.float32)
           for n, o in enumerate((oa_ref, ob_ref, oc_ref))]
    cols = []
    for c in range(N_HEADS):
        sl = slice(c * LANES, (c + 1) * LANES)
        mc = ups[0][:, sl] * g_ref[0, 0, c].astype(jnp.float32)
        for n in range(1, N_BRANCHES):
            mc = mc + ups[n][:, sl] * g_ref[n, 0, c].astype(jnp.float32)
        cols.append(mc.astype(jnp.bfloat16))
    merged = jnp.concatenate(cols, axis=1)
    y = jnp.dot(merged, wo_ref[...], preferred_element_type=jnp.float32)
    o_ref[0] = _layer_norm(alpha * x_ref[0] + y, gam_ref[...], bet_ref[...])


def _merge(oa, ob, oc, pq5, x, wb, wo, gam, bet, alpha, t=512):
    b, s, d = x.shape
    o_spec = pl.BlockSpec((1, t, BRANCH), lambda bi, si: (bi, si, 0))
    return pl.pallas_call(
        functools.partial(_merge_kernel, alpha=alpha),
        grid=(b, s // t),
        in_specs=[o_spec, o_spec, o_spec,
                  pl.BlockSpec((N_BRANCHES, 1, N_HEADS, t, LANES), lambda bi, si: (G_GATES // N_BRANCHES, bi, 0, si, 0)),
                  pl.BlockSpec((1, t, d), lambda bi, si: (bi, si, 0)),
                  pl.BlockSpec((N_BRANCHES, BRANCH, d), lambda bi, si: (0, 0, 0)),
                  pl.BlockSpec((d, d), lambda bi, si: (0, 0)),
                  pl.BlockSpec((1, d), lambda bi, si: (0, 0)),
                  pl.BlockSpec((1, d), lambda bi, si: (0, 0))],
        out_specs=pl.BlockSpec((1, t, d), lambda bi, si: (bi, si, 0)),
        out_shape=jax.ShapeDtypeStruct((b, s, d), jnp.float32),
        compiler_params=_params(("arbitrary", "arbitrary"), 48),
        name="merge",
    )(oa, ob, oc, pq5, x, wb, wo, gam, bet)


def _ffn_kernel(x_ref, w1_ref, w2_ref, gam_ref, bet_ref, o_ref, *, alpha, fchunk):
    x = x_ref[0]
    xb = x.astype(jnp.bfloat16)
    y = jnp.zeros_like(x)
    for c in range(w1_ref.shape[1] // fchunk):
        sl = slice(c * fchunk, (c + 1) * fchunk)
        h = jnp.dot(xb, w1_ref[:, sl], preferred_element_type=jnp.float32)
        h = jnp.square(jnp.maximum(h, 0.0)).astype(jnp.bfloat16)
        y = y + jnp.dot(h, w2_ref[sl, :], preferred_element_type=jnp.float32)
    o_ref[0] = _layer_norm(alpha * x + y, gam_ref[...], bet_ref[...])


def _ffn(x, w1, w2, gam, bet, alpha, t=512):
    b, s, d = x.shape
    f = w1.shape[1]
    return pl.pallas_call(
        functools.partial(_ffn_kernel, alpha=alpha, fchunk=1024),
        grid=(b, s // t),
        in_specs=[pl.BlockSpec((1, t, d), lambda bi, si: (bi, si, 0)),
                  pl.BlockSpec((d, f), lambda bi, si: (0, 0)),
                  pl.BlockSpec((f, d), lambda bi, si: (0, 0)),
                  pl.BlockSpec((1, d), lambda bi, si: (0, 0)),
                  pl.BlockSpec((1, d), lambda bi, si: (0, 0))],
        out_specs=pl.BlockSpec((1, t, d), lambda bi, si: (bi, si, 0)),
        out_shape=jax.ShapeDtypeStruct((b, s, d), jnp.float32),
        compiler_params=_params(("arbitrary", "arbitrary"), 56),
        name="ffn",
    )(x, w1, w2, gam, bet)


def _aug_weights(slopes):
    wa = np.zeros((N_BIAS_GROUPS, LANES, N_HEADS * LANES), np.float32)
    for h in range(N_HEADS):
        base = h * LANES + AUG_LANE
        sl = float(slopes[h])
        wa[G_DSA_Q, XE_A, base + 0] = -sl * CHUNK
        wa[G_DSA_Q, XE_B, base + 1] = -sl
        wa[G_DSA_Q, XE_ONE, base + 2] = 1.0
        wa[G_DSA_Q, XE_ONE, base + 3] = 1.0
        wa[G_DSA_K, XE_ONE, base + 0] = 1.0
        wa[G_DSA_K, XE_ONE, base + 1] = 1.0
        wa[G_DSA_K, XE_A, base + 2] = sl * CHUNK
        wa[G_DSA_K, XE_B, base + 3] = sl
        for p in range(3):
            wa[G_FOX_Q, XE_ONE, base + p] = 1.0
            wa[G_FOX_K, XE_C + N_HEADS * p + h, base + p] = -1.0
    return wa


def _in_offsets(d):
    sizes = (BRANCH, BRANCH, BRANCH, IDX_HEADS * IDX_DIM, IDX_DIM, IDX_HEADS,
             BRANCH, BRANCH, BRANCH, N_HEADS, BRANCH, BRANCH, BRANCH, N_BRANCHES * d)
    names = ("qa", "ka", "va", "qi", "ki", "wi", "qf", "kf", "vf", "fg", "qs", "ks", "vs", "gates")
    return dict(zip(names, np.concatenate([[0], np.cumsum(sizes)[:-1]]).tolist())), int(sum(sizes))


def _wlayout_kernel(w_ref, tail_ref, main_ref, small_ref, vt_ref, *, offs, aligned_cols):
    scale = HEAD_DIM ** -0.5

    def window(lo, width):
        return w_ref[0, :, lo:lo + width].astype(jnp.bfloat16)

    def span(sec, n_src):
        lo = sec // LANES * LANES
        return lo, sec - lo, -(-(sec - lo + n_src) // LANES) * LANES

    def shifted(src, width, n, off, padded):
        r = lax.broadcasted_iota(jnp.int32, (width, n), 0)
        c = lax.broadcasted_iota(jnp.int32, (width, n), 1)
        row = off + c
        if padded:
            row = jnp.where((c & (LANES - 1)) < HEAD_DIM, off + (c >> 7) * HEAD_DIM + (c & (LANES - 1)), -1)
        return jnp.dot(src, jnp.where(r == row, 1.0, 0.0).astype(jnp.bfloat16), preferred_element_type=jnp.float32)

    def section(sec, n_src, n, padded):
        lo, off, width = span(sec, n_src)
        if lo + width <= aligned_cols:
            return shifted(window(lo, width), width, n, off, padded)
        main = shifted(window(lo, aligned_cols - lo), aligned_cols - lo, n, off, padded)
        return main + shifted(tail_ref[0].astype(jnp.bfloat16), LANES, n, off - (aligned_cols - lo), padded)

    wide = N_HEADS * LANES
    main_ref[0, G_DSA_Q] = (section(offs["qa"], BRANCH, wide, True) * scale).astype(jnp.bfloat16)
    main_ref[0, G_DSA_K] = section(offs["ka"], BRANCH, wide, True).astype(jnp.bfloat16)
    main_ref[0, G_FOX_Q] = (section(offs["qf"], BRANCH, wide, True) * scale).astype(jnp.bfloat16)
    main_ref[0, G_FOX_K] = section(offs["kf"], BRANCH, wide, True).astype(jnp.bfloat16)
    qk = section(offs["qs"], 2 * BRANCH, wide, False)
    col = lax.broadcasted_iota(jnp.int32, qk.shape, 1)
    main_ref[0, G_SB] = jnp.where(col < BRANCH, qk * scale, qk).astype(jnp.bfloat16)
    main_ref[0, G_QI] = section(offs["qi"], wide, wide, False).astype(jnp.bfloat16)
    for n in range(N_BRANCHES):
        main_ref[0, G_GATES + n] = section(offs["gates"] + n * wide, wide, wide, False).astype(jnp.bfloat16)

    def small_piece(sec, n_src, dst):
        lo, off, width = span(sec, n_src)
        r = lax.broadcasted_iota(jnp.int32, (width, 2 * LANES), 0)
        c = lax.broadcasted_iota(jnp.int32, (width, 2 * LANES), 1)
        row = jnp.where(c >= dst, jnp.where(c < dst + n_src, off + c - dst, -1), -1)
        return jnp.dot(window(lo, width), jnp.where(r == row, 1.0, 0.0).astype(jnp.bfloat16),
                       preferred_element_type=jnp.float32)

    small = (small_piece(offs["fg"], N_HEADS, 0) + small_piece(offs["wi"], IDX_HEADS, N_HEADS)
             + small_piece(offs["ki"], IDX_DIM, LANES) + small_piece(offs["ki"], IDX_DIM, LANES + IDX_DIM))
    small_ref[0] = small.astype(jnp.bfloat16)

    for n, name in enumerate(("va", "vf", "vs")):
        lo, off, width = span(offs[name], BRANCH)
        r = lax.broadcasted_iota(jnp.int32, (BRANCH, width), 0)
        c = lax.broadcasted_iota(jnp.int32, (BRANCH, width), 1)
        pick = jnp.where(c == off + r, 1.0, 0.0).astype(jnp.bfloat16)
        vt_ref[0, n * BRANCH:(n + 1) * BRANCH, :] = lax.dot_general(
            pick, window(lo, width), _NT, preferred_element_type=jnp.float32).astype(jnp.bfloat16)


def _wlayout(w_in, t=256):
    depth, d, cols = w_in.shape
    offs, total = _in_offsets(d)
    assert total == cols
    aligned = cols // LANES * LANES
    tail = jnp.pad(w_in[:, :, aligned:], ((0, 0), (0, 0), (0, LANES - (cols - aligned))))
    wide = N_HEADS * LANES
    return pl.pallas_call(
        functools.partial(_wlayout_kernel, offs=offs, aligned_cols=aligned),
        grid=(depth, d // t),
        in_specs=[pl.BlockSpec((1, t, cols), lambda l, r: (l, r, 0)),
                  pl.BlockSpec((1, t, LANES), lambda l, r: (l, r, 0))],
        out_specs=[pl.BlockSpec((1, N_GROUPS, t, wide), lambda l, r: (l, 0, r, 0)),
                   pl.BlockSpec((1, t, 2 * LANES), lambda l, r: (l, r, 0)),
                   pl.BlockSpec((1, N_BRANCHES * BRANCH, t), lambda l, r: (l, 0, r))],
        out_shape=[jax.ShapeDtypeStruct((depth, N_GROUPS, d, wide), jnp.bfloat16),
                   jax.ShapeDtypeStruct((depth, d, 2 * LANES), jnp.bfloat16),
                   jax.ShapeDtypeStruct((depth, N_BRANCHES * BRANCH, d), jnp.bfloat16)],
        compiler_params=_params(("arbitrary", "arbitrary"), 56),
        name="wlayout",
    )(w_in, tail)


def kernel(x, w_in, b_forget, w_branch, w_out, ln1_g, ln1_b, w_ff1, w_ff2, ln2_g, ln2_b):
    depth = w_in.shape[0]
    b, s, d = x.shape
    assert s % CHUNK == 0 and d == N_HEADS * LANES
    alpha = (2.0 * depth) ** 0.25
    topk = min(TOPK_MAX, s // 4)
    slopes = 2.0 ** (-8.0 * np.arange(1, N_HEADS + 1, dtype=np.float64) / N_HEADS)
    w_aug_np = _aug_weights(slopes)
    assert np.array_equal(w_aug_np, w_aug_np.astype(jnp.bfloat16).astype(np.float32))
    w_aug = jnp.asarray(w_aug_np, jnp.bfloat16)
    nc = s // CHUNK

    w_main, w_small, w_vt = _wlayout(w_in)
    for layer in range(depth):
        b_small = jnp.pad(b_forget[layer], (0, LANES - N_HEADS)).reshape(1, LANES)
        xb, xe, small, ki = _prep(x, w_small, layer, b_small)
        pq5 = _proj(xb, xe, w_main, layer, w_aug)
        pq = pq5.reshape(N_GROUPS, b, N_HEADS, nc, CHUNK, LANES)
        vt = _proj_vt(xb, w_vt, layer)
        wt = small[:, :, N_HEADS:N_HEADS + IDX_HEADS].transpose(0, 2, 1)
        o_a = _dsa(pq, vt, ki.reshape(b, nc, CHUNK, LANES), wt, topk)
        o_b = _causal_attention(_fox_kernel, "fox", _softmax_scratch, pq, vt,
                                (G_FOX_Q, 0, N_HEADS), (G_FOX_K, 0, N_HEADS), 1)
        o_c = _causal_attention(_sb_kernel, "sb", _sb_scratch, pq, vt,
                                (G_SB, 0, N_HEADS // 2), (G_SB, 1, N_HEADS // 2), 2)
        x = _merge(o_a, o_b, o_c, pq5, x, w_branch[layer].astype(jnp.bfloat16), w_out[layer].astype(jnp.bfloat16),
                   ln1_g[layer].reshape(1, d), ln1_b[layer].reshape(1, d), alpha)
        x = _ffn(x, w_ff1[layer].astype(jnp.bfloat16), w_ff2[layer].astype(jnp.bfloat16),
                 ln2_g[layer].reshape(1, d), ln2_b[layer].reshape(1, d), alpha)
    return x
```

```python
import functools

import numpy as np
import jax
import jax.numpy as jnp
from jax import lax
from jax.experimental import pallas as pl
from jax.experimental.pallas import tpu as pltpu

HEAD_DIM = 64
N_HEADS = 8
BRANCH = N_HEADS * HEAD_DIM
IDX_HEADS = 16
IDX_DIM = 64
TOPK_MAX = 256
N_BRANCHES = 3
LN_EPS = 1e-5

LANES = 128
CHUNK = 256
N_GROUPS = 9
G_DSA_Q, G_DSA_K, G_FOX_Q, G_FOX_K, G_SB, G_QI, G_GATES = 0, 1, 2, 3, 4, 5, 6
N_BIAS_GROUPS = 4
PROJ_BLOCK_LANES = 2
AUG_LANE = HEAD_DIM
XE_A, XE_B, XE_ONE, XE_C = 0, 1, 2, 3
NEG_BIG = -1e30
EXP_IS_ZERO_BELOW = -105.0
INT_MIN = -(2 ** 31)
KEY_OF_LOWEST_FINITE = INT_MIN + 2 ** 23

_NT = (((1,), (1,)), ((), ()))


def _log_sigmoid(x):
    return jnp.minimum(x, 0.0) - jnp.log(1.0 + jnp.exp(-jnp.abs(x)))


def _split3(x):
    a = x.astype(jnp.bfloat16)
    r = x - a.astype(jnp.float32)
    b = r.astype(jnp.bfloat16)
    c = (r - b.astype(jnp.float32)).astype(jnp.bfloat16)
    return a, b, c


def _layer_norm(z, g, b):
    mu = jnp.mean(z, axis=-1, keepdims=True)
    zc = z - mu
    var = jnp.mean(zc * zc, axis=-1, keepdims=True)
    return zc * lax.rsqrt(var + LN_EPS) * g + b


def _params(sem, vmem_mb):
    return pltpu.CompilerParams(dimension_semantics=sem, vmem_limit_bytes=vmem_mb * 1024 * 1024)


def _prep_kernel(x_ref, ws_ref, bs_ref, xb_ref, xe_ref, sm_ref, ki_ref, carry_ref):
    s_blk = pl.program_id(1)
    t = x_ref.shape[1]
    xb = x_ref[0].astype(jnp.bfloat16)
    xb_ref[0] = xb
    both = jnp.dot(xb, ws_ref[0], preferred_element_type=jnp.float32)
    sm = both[:, :LANES]
    sm_ref[0] = sm
    ki_ref[0] = both[:, LANES:].astype(jnp.bfloat16)
    fg = sm + bs_ref[...]
    lf = jnp.minimum(fg, 0.0) - jnp.log1p(jnp.exp(-jnp.abs(fg)))

    @pl.when(s_blk == 0)
    def _():
        carry_ref[...] = jnp.zeros_like(carry_ref)

    row = lax.broadcasted_iota(jnp.int32, (t, t), 0)
    col = lax.broadcasted_iota(jnp.int32, (t, t), 1)
    tri = jnp.where(col <= row, 1.0, 0.0).astype(jnp.bfloat16)
    cs = carry_ref[0:1, :]
    for part in _split3(lf):
        cs = cs + jnp.dot(tri, part, preferred_element_type=jnp.float32)
    carry_ref[...] = jnp.broadcast_to(cs[t - 1:t, :], carry_ref.shape)

    er = lax.broadcasted_iota(jnp.int32, (LANES, LANES), 0)
    ec = lax.broadcasted_iota(jnp.int32, (LANES, LANES), 1)
    xe = jnp.zeros((t, LANES), jnp.float32)
    for p, part in enumerate(_split3(cs)):
        place = jnp.where(ec == er + (XE_C + N_HEADS * p), jnp.where(er < N_HEADS, 1.0, 0.0), 0.0)
        xe = xe + jnp.dot(part, place.astype(jnp.bfloat16), preferred_element_type=jnp.float32)
    pos = s_blk * t + lax.broadcasted_iota(jnp.int32, (t, LANES), 0)
    lane = lax.broadcasted_iota(jnp.int32, (t, LANES), 1)
    xe = jnp.where(lane == XE_A, jnp.right_shift(pos, 8).astype(jnp.float32), xe)
    xe = jnp.where(lane == XE_B, (pos & (CHUNK - 1)).astype(jnp.float32), xe)
    xe = jnp.where(lane == XE_ONE, 1.0, xe)
    xe_ref[0] = xe.astype(jnp.bfloat16)


def _prep(x, w_small, layer, b_small, t=512):
    b, s, d = x.shape
    return pl.pallas_call(
        _prep_kernel,
        grid=(b, s // t),
        in_specs=[pl.BlockSpec((1, t, d), lambda bi, si: (bi, si, 0)),
                  pl.BlockSpec((1, d, 2 * LANES), lambda bi, si: (layer, 0, 0)),
                  pl.BlockSpec((1, LANES), lambda bi, si: (0, 0))],
        out_specs=[pl.BlockSpec((1, t, d), lambda bi, si: (bi, si, 0)),
                   pl.BlockSpec((1, t, LANES), lambda bi, si: (bi, si, 0)),
                   pl.BlockSpec((1, t, LANES), lambda bi, si: (bi, si, 0)),
                   pl.BlockSpec((1, t, LANES), lambda bi, si: (bi, si, 0))],
        out_shape=[jax.ShapeDtypeStruct((b, s, d), jnp.bfloat16),
                   jax.ShapeDtypeStruct((b, s, LANES), jnp.bfloat16),
                   jax.ShapeDtypeStruct((b, s, LANES), jnp.float32),
                   jax.ShapeDtypeStruct((b, s, LANES), jnp.bfloat16)],
        scratch_shapes=[pltpu.VMEM((8, LANES), jnp.float32)],
        compiler_params=_params(("arbitrary", "arbitrary"), 32),
        name="prep",
    )(x, w_small, b_small)


def _proj_kernel(xb_ref, xe_ref, w_ref, wa_ref, o_ref):
    j = pl.program_id(0)

    def column_blocks(with_bias):
        is_gate = jnp.logical_and(j >= G_GATES, j < G_GATES + N_BRANCHES)
        width = PROJ_BLOCK_LANES * LANES
        for c in range(N_HEADS // PROJ_BLOCK_LANES):
            sl = slice(c * width, (c + 1) * width)
            acc = jnp.dot(xb_ref[0], w_ref[0, 0, :, sl], preferred_element_type=jnp.float32)
            if with_bias:
                acc = acc + jnp.dot(xe_ref[0], wa_ref[0, :, sl], preferred_element_type=jnp.float32)
            else:
                acc = jnp.where(is_gate, jax.nn.sigmoid(acc), acc)
            for g in range(PROJ_BLOCK_LANES):
                o_ref[0, 0, PROJ_BLOCK_LANES * c + g] = acc[:, g * LANES:(g + 1) * LANES].astype(jnp.bfloat16)

    @pl.when(j < N_BIAS_GROUPS)
    def _():
        column_blocks(True)

    @pl.when(j >= N_BIAS_GROUPS)
    def _():
        column_blocks(False)


def _proj(xb, xe, w_main, layer, w_aug, t=1024):
    b, s, d = xb.shape
    assert s % t == 0
    ns = s // t
    return pl.pallas_call(
        _proj_kernel,
        grid=(N_GROUPS, b * ns),
        in_specs=[pl.BlockSpec((1, t, d), lambda j, i: (i // ns, i % ns, 0)),
                  pl.BlockSpec((1, t, LANES), lambda j, i: (i // ns, i % ns, 0)),
                  pl.BlockSpec((1, 1, d, N_HEADS * LANES), lambda j, i: (layer, j, 0, 0)),
                  pl.BlockSpec((1, LANES, N_HEADS * LANES), lambda j, i: (jnp.minimum(j, N_BIAS_GROUPS - 1), 0, 0))],
        out_specs=pl.BlockSpec((1, 1, N_HEADS, t, LANES), lambda j, i: (j, i // ns, 0, i % ns, 0)),
        out_shape=jax.ShapeDtypeStruct((N_GROUPS, b, N_HEADS, s, LANES), jnp.bfloat16),
        compiler_params=_params(("arbitrary", "arbitrary"), 48),
        name="proj",
    )(xb, xe, w_main, w_aug)


def _proj_vt_kernel(xb_ref, w_ref, o_ref):
    vt = lax.dot_general(w_ref[0], xb_ref[0], _NT, preferred_element_type=jnp.float32)
    for c in range(xb_ref.shape[1] // CHUNK):
        blk = vt[:, c * CHUNK:(c + 1) * CHUNK].astype(jnp.bfloat16)
        o_ref[0, :, c] = blk.reshape(N_BRANCHES * N_HEADS, HEAD_DIM, CHUNK)


def _proj_vt(xb, w_vt, layer, t=1024):
    b, s, d = xb.shape
    nh = N_BRANCHES * N_HEADS
    return pl.pallas_call(
        _proj_vt_kernel,
        grid=(b, s // t),
        in_specs=[pl.BlockSpec((1, t, d), lambda bi, si: (bi, si, 0)),
                  pl.BlockSpec((1, nh * HEAD_DIM, d), lambda bi, si: (layer, 0, 0))],
        out_specs=pl.BlockSpec((1, nh, t // CHUNK, HEAD_DIM, CHUNK), lambda bi, si: (bi, 0, si, 0, 0)),
        out_shape=jax.ShapeDtypeStruct((b, nh, s // CHUNK, HEAD_DIM, CHUNK), jnp.bfloat16),
        compiler_params=_params(("arbitrary", "arbitrary"), 48),
        name="proj_vt",
    )(xb, w_vt)


STAGE_LAG = 8
COUNT_ACCS = 4


def _staged(stages):
    for step in range(N_HEADS + STAGE_LAG * (len(stages) - 1)):
        for n, stage in enumerate(stages):
            if 0 <= step - STAGE_LAG * n < N_HEADS:
                stage(step - STAGE_LAG * n)


def _softmax_tile(j0, n, q_ref, k_ref, vt_ref, mask, refs):
    s_ref, mloc_ref, m_ref, l_ref, acc_ref = refs
    rows = n * CHUNK

    def scores(h):
        keys = k_ref[0, 0, h, pl.ds(j0, n)].reshape(rows, LANES)
        s = mask(lax.dot_general(keys, q_ref[0, 0, h, 0], _NT, preferred_element_type=jnp.float32))
        s_ref[h, :rows] = s
        mloc_ref[h] = jnp.max(s, axis=0, keepdims=True)

    def update(h):
        m = m_ref[h]
        m_new = jnp.maximum(m, mloc_ref[h])
        alpha = jnp.exp(m - m_new)
        p = jnp.exp(s_ref[h, :rows] - m_new)
        m_ref[h] = m_new
        l_ref[h] = alpha * l_ref[h] + jnp.sum(p, axis=0, keepdims=True)
        pb = p.astype(jnp.bfloat16)
        pv = jnp.dot(vt_ref[0, h, j0], pb[:CHUNK], preferred_element_type=jnp.float32)
        for c in range(1, n):
            pv = pv + jnp.dot(vt_ref[0, h, j0 + c], pb[c * CHUNK:(c + 1) * CHUNK], preferred_element_type=jnp.float32)
        acc_ref[h] = alpha * acc_ref[h] + pv

    _staged([scores, update])


def _split_head_pairs(q_ref, qz_ref, n_heads):
    tq = q_ref.shape[4]
    lane = lax.broadcasted_iota(jnp.int32, (tq, LANES), 1)
    for p in range(n_heads // 2):
        qp = q_ref[0, 0, p, 0]
        qz_ref[2 * p] = jnp.where(lane < HEAD_DIM, qp, jnp.zeros_like(qp))
        qz_ref[2 * p + 1] = jnp.where(lane >= HEAD_DIM, qp, jnp.zeros_like(qp))


def _store_heads(o_ref, head_out):
    for p in range(N_HEADS // 2):
        pair = jnp.concatenate([head_out(2 * p), head_out(2 * p + 1)], axis=0)
        o_ref[0, :, p * LANES:(p + 1) * LANES] = pair.T.astype(o_ref.dtype)


def _softmax_init(m_ref, l_ref, acc_ref):
    m_ref[...] = jnp.full(m_ref.shape, NEG_BIG, jnp.float32)
    l_ref[...] = jnp.zeros(l_ref.shape, jnp.float32)
    acc_ref[...] = jnp.zeros(acc_ref.shape, jnp.float32)


TILE_CHUNKS = 4


def _softmax_scratch(tq):
    vec = pltpu.VMEM((N_HEADS, 1, tq), jnp.float32)
    return [pltpu.VMEM((N_HEADS, TILE_CHUNKS * CHUNK, tq), jnp.float32), vec, vec, vec,
            pltpu.VMEM((N_HEADS, HEAD_DIM, tq), jnp.float32)]


def _softmax_chunks(n_chunks, tile):
    n_tiles = n_chunks // TILE_CHUNKS

    def body(t, _):
        tile(t * TILE_CHUNKS, TILE_CHUNKS)
        return 0

    lax.fori_loop(0, n_tiles, body, 0)
    done = n_tiles * TILE_CHUNKS
    size = TILE_CHUNKS // 2
    while size >= 1:
        has = ((n_chunks - done) & size) != 0

        @pl.when(has)
        def _(done=done, size=size):
            tile(done, size)

        done = done + jnp.where(has, size, 0)
        size //= 2


def _dsa_kernel(q_ref, k_ref, vt_ref, qi_ref, ki_ref, wt_ref, o_ref, sc_ref, qz_ref,
                s_ref, mloc_ref, m_ref, l_ref, acc_ref, *, topk):
    i = pl.program_id(1)
    tq = q_ref.shape[4]
    _split_head_pairs(qi_ref, qz_ref, IDX_HEADS)
    rel = (lax.broadcasted_iota(jnp.int32, (CHUNK, tq), 0)
           - lax.broadcasted_iota(jnp.int32, (CHUNK, tq), 1))

    def score_chunks(j0, n):
        kc = ki_ref[0, pl.ds(j0, n)].reshape(n * CHUNK, LANES)
        sc = jnp.zeros((n * CHUNK, tq), jnp.float32)
        for h in range(IDX_HEADS):
            d = lax.dot_general(kc, qz_ref[h], _NT, preferred_element_type=jnp.float32)
            sc = sc + jnp.maximum(d, 0.0) * wt_ref[0, h:h + 1, :]
        for c in range(n):
            sc_ref[j0 + c] = jnp.where(rel <= (i - j0 - c) * CHUNK, sc[c * CHUNK:(c + 1) * CHUNK], -jnp.inf)

    def score_pair(t, _):
        score_chunks(2 * t, 2)
        return 0

    lax.fori_loop(0, (i + 1) // 2, score_pair, 0)

    @pl.when((i + 1) % 2 == 1)
    def _():
        score_chunks(i, 1)

    def ordered_float(key):
        return pltpu.bitcast(key ^ ((key >> 31) & jnp.int32(0x7FFFFFFF)), jnp.float32)

    def count_ge(cand):
        def body(j, acc):
            ind = jnp.where(sc_ref[j] >= cand, 1.0, 0.0)
            return acc + jnp.sum(ind.reshape(CHUNK // (8 * COUNT_ACCS), COUNT_ACCS * 8, tq), axis=0)
        acc = lax.fori_loop(0, i + 1, body, jnp.zeros((COUNT_ACCS * 8, tq), jnp.float32))
        return jnp.sum(acc, axis=0, keepdims=True)

    key = jnp.where(count_ge(jnp.zeros((1, tq), jnp.float32)) >= topk, jnp.int32(0), jnp.int32(INT_MIN))

    def bit_body(n, key):
        cand = key + jnp.left_shift(jnp.int32(1), 30 - n)
        return jnp.where(count_ge(ordered_float(cand)) >= topk, cand, key)

    key = lax.fori_loop(0, 31, bit_body, key)
    thr = ordered_float(jnp.maximum(key, jnp.int32(KEY_OF_LOWEST_FINITE)))

    _softmax_init(m_ref, l_ref, acc_ref)

    def tile(j0, n):
        selected = sc_ref[pl.ds(j0, n)].reshape(n * CHUNK, tq) >= thr
        _softmax_tile(j0, n, q_ref, k_ref, vt_ref, lambda s: jnp.where(selected, s, NEG_BIG),
                      (s_ref, mloc_ref, m_ref, l_ref, acc_ref))

    _softmax_chunks(i + 1, tile)
    _store_heads(o_ref, lambda h: acc_ref[h] / l_ref[h])


def _dsa(pq, vt, ki, wt, topk):
    _, b, _, nc, _, _ = pq.shape
    tq = CHUNK
    kern = functools.partial(_dsa_kernel, topk=float(topk))
    return pl.pallas_call(
        kern,
        grid=(b, nc),
        in_specs=[pl.BlockSpec((1, 1, N_HEADS, 1, tq, LANES), lambda bi, i: (G_DSA_Q, bi, 0, i, 0, 0)),
                  pl.BlockSpec((1, 1, N_HEADS, nc, CHUNK, LANES), lambda bi, i: (G_DSA_K, bi, 0, 0, 0, 0)),
                  pl.BlockSpec((1, N_HEADS, nc, HEAD_DIM, CHUNK), lambda bi, i: (bi, 0, 0, 0, 0)),
                  pl.BlockSpec((1, 1, N_HEADS, 1, tq, LANES), lambda bi, i: (G_QI, bi, 0, i, 0, 0)),
                  pl.BlockSpec((1, nc, CHUNK, LANES), lambda bi, i: (bi, 0, 0, 0)),
                  pl.BlockSpec((1, IDX_HEADS, tq), lambda bi, i: (bi, 0, i))],
        out_specs=pl.BlockSpec((1, tq, BRANCH), lambda bi, i: (bi, i, 0)),
        out_shape=jax.ShapeDtypeStruct((b, nc * CHUNK, BRANCH), jnp.bfloat16),
        scratch_shapes=[pltpu.VMEM((nc, CHUNK, tq), jnp.float32),
                        pltpu.VMEM((IDX_HEADS, tq, LANES), jnp.bfloat16)] + _softmax_scratch(tq),
        compiler_params=_params(("arbitrary", "arbitrary"), 56),
        name="dsa",
    )(pq, pq, vt, pq, ki, wt)


def _fox_kernel(q_ref, k_ref, vt_ref, o_ref, s_ref, mloc_ref, m_ref, l_ref, acc_ref):
    i = pl.program_id(1)
    tq = q_ref.shape[4]
    causal = (lax.broadcasted_iota(jnp.int32, (CHUNK, tq), 0)
              <= lax.broadcasted_iota(jnp.int32, (CHUNK, tq), 1))
    _softmax_init(m_ref, l_ref, acc_ref)
    refs = (s_ref, mloc_ref, m_ref, l_ref, acc_ref)
    _softmax_chunks(i, lambda j0, n: _softmax_tile(j0, n, q_ref, k_ref, vt_ref, lambda s: s, refs))
    _softmax_tile(i, 1, q_ref, k_ref, vt_ref, lambda s: jnp.where(causal, s, NEG_BIG), refs)
    _store_heads(o_ref, lambda h: acc_ref[h] / l_ref[h])


def _sb_scratch(tq):
    tile = pltpu.VMEM((N_HEADS, CHUNK, tq), jnp.bfloat16)
    vec = pltpu.VMEM((N_HEADS, 1, tq), jnp.float32)
    return [pltpu.VMEM((N_HEADS, tq, LANES), jnp.bfloat16),
            pltpu.VMEM((N_HEADS, CHUNK, tq), jnp.float32), tile, tile, tile, vec, vec,
            pltpu.VMEM((N_HEADS, HEAD_DIM, tq), jnp.float32)]


def _sb_kernel(q_ref, k_ref, vt_ref, o_ref, qz_ref, lb_ref, hi_ref, lo_ref, a_ref, later_ref, lom0_ref, acc_ref):
    i = pl.program_id(1)
    tq = q_ref.shape[4]
    strict = (lax.broadcasted_iota(jnp.int32, (CHUNK, tq), 0)
              < lax.broadcasted_iota(jnp.int32, (CHUNK, tq), 1))
    upper = jnp.where(lax.broadcasted_iota(jnp.int32, (CHUNK, CHUNK), 1)
                      > lax.broadcasted_iota(jnp.int32, (CHUNK, CHUNK), 0), 1.0, 0.0).astype(jnp.bfloat16)
    later_ref[...] = jnp.zeros(later_ref.shape, jnp.float32)
    acc_ref[...] = jnp.zeros(acc_ref.shape, jnp.float32)
    _split_head_pairs(q_ref, qz_ref, N_HEADS)

    def chunk(j, diag):
        def logits(h):
            z = lax.dot_general(k_ref[0, 0, h // 2, j], qz_ref[h], _NT, preferred_element_type=jnp.float32)
            log_beta = _log_sigmoid(z)
            lom = log_beta - z
            if diag:
                lom = jnp.where(strict, lom, 0.0)
            hi = pltpu.bitcast(pltpu.bitcast(lom, jnp.int32) & jnp.int32(-65536), jnp.float32)
            lb_ref[h] = log_beta
            hi_ref[h] = hi.astype(jnp.bfloat16)
            lo_ref[h] = (lom - hi).astype(jnp.bfloat16)
            lom0_ref[h] = lom[0:1, :]

        def weights(h):
            within = (jnp.dot(upper, hi_ref[h], preferred_element_type=jnp.float32)
                      + jnp.dot(upper, lo_ref[h], preferred_element_type=jnp.float32))
            a = jnp.exp(lb_ref[h] + within + later_ref[h])
            if diag:
                a = jnp.where(strict, a, 0.0)
            a_ref[h] = a.astype(jnp.bfloat16)
            later_ref[h] += within[0:1, :] + lom0_ref[h]

        def values(h):
            acc_ref[h] += jnp.dot(vt_ref[0, h, j], a_ref[h], preferred_element_type=jnp.float32)

        _staged([logits, weights, values])
        return 0

    chunk(i, True)

    def live(state):
        n, later_max = state
        return jnp.logical_and(n < i, later_max > EXP_IS_ZERO_BELOW)

    def step(state):
        n, _ = state
        chunk(i - 1 - n, False)
        return n + 1, jnp.max(later_ref[...])

    lax.while_loop(live, step, (jnp.int32(0), jnp.max(later_ref[...])))
    _store_heads(o_ref, lambda h: acc_ref[h])


def _causal_attention(kernel, name, scratch, pq, vt, q_at, k_at, branch):
    _, b, _, nc, _, _ = pq.shape
    tq = CHUNK
    (g_q, q_blk, q_n), (g_k, k_blk, k_n) = q_at, k_at
    return pl.pallas_call(
        kernel,
        grid=(b, nc),
        in_specs=[pl.BlockSpec((1, 1, q_n, 1, tq, LANES), lambda bi, i: (g_q, bi, q_blk, i, 0, 0)),
                  pl.BlockSpec((1, 1, k_n, nc, CHUNK, LANES), lambda bi, i: (g_k, bi, k_blk, 0, 0, 0)),
                  pl.BlockSpec((1, N_HEADS, nc, HEAD_DIM, CHUNK), lambda bi, i: (bi, branch, 0, 0, 0))],
        out_specs=pl.BlockSpec((1, tq, BRANCH), lambda bi, i: (bi, i, 0)),
        out_shape=jax.ShapeDtypeStruct((b, nc * CHUNK, BRANCH), jnp.bfloat16),
        scratch_shapes=scratch(tq),
        compiler_params=_params(("arbitrary", "arbitrary"), 48),
        name=name,
    )(pq, pq, vt)


def _merge_kernel(oa_ref, ob_ref, oc_ref, g_ref, x_ref, wb_ref, wo_ref, gam_ref, bet_ref, o_ref, *, alpha):
    ups = [jnp.dot(o[0], wb_ref[0, n], preferred_element_type=jnp.float32)
           for n, o in enumerate((oa_ref, ob_ref, oc_ref))]
    cols = []
    for c in range(N_HEADS):
        sl = slice(c * LANES, (c + 1) * LANES)
        mc = ups[0][:, sl] * g_ref[0, 0, c].astype(jnp.float32)
        for n in range(1, N_BRANCHES):
            mc = mc + ups[n][:, sl] * g_ref[n, 0, c].astype(jnp.float32)
        cols.append(mc.astype(jnp.bfloat16))
    merged = jnp.concatenate(cols, axis=1)
    y = jnp.dot(merged, wo_ref[0], preferred_element_type=jnp.float32)
    o_ref[0] = _layer_norm(alpha * x_ref[0] + y, gam_ref[...], bet_ref[...])


def _merge(oa, ob, oc, pq5, x, wb, wo, layer, gam, bet, alpha, t=512):
    b, s, d = x.shape
    o_spec = pl.BlockSpec((1, t, BRANCH), lambda bi, si: (bi, si, 0))
    return pl.pallas_call(
        functools.partial(_merge_kernel, alpha=alpha),
        grid=(b, s // t),
        in_specs=[o_spec, o_spec, o_spec,
                  pl.BlockSpec((N_BRANCHES, 1, N_HEADS, t, LANES), lambda bi, si: (G_GATES // N_BRANCHES, bi, 0, si, 0)),
                  pl.BlockSpec((1, t, d), lambda bi, si: (bi, si, 0)),
                  pl.BlockSpec((1, N_BRANCHES, BRANCH, d), lambda bi, si: (layer, 0, 0, 0)),
                  pl.BlockSpec((1, d, d), lambda bi, si: (layer, 0, 0)),
                  pl.BlockSpec((1, d), lambda bi, si: (0, 0)),
                  pl.BlockSpec((1, d), lambda bi, si: (0, 0))],
        out_specs=pl.BlockSpec((1, t, d), lambda bi, si: (bi, si, 0)),
        out_shape=jax.ShapeDtypeStruct((b, s, d), jnp.float32),
        compiler_params=_params(("arbitrary", "arbitrary"), 48),
        name="merge",
    )(oa, ob, oc, pq5, x, wb, wo, gam, bet)


def _ffn_kernel(x_ref, w1_ref, w2_ref, gam_ref, bet_ref, o_ref, *, alpha, fchunk):
    x = x_ref[0]
    xb = x.astype(jnp.bfloat16)
    y = jnp.zeros_like(x)
    for c in range(w1_ref.shape[2] // fchunk):
        sl = slice(c * fchunk, (c + 1) * fchunk)
        h = jnp.dot(xb, w1_ref[0, :, sl], preferred_element_type=jnp.float32)
        h = jnp.square(jnp.maximum(h, 0.0)).astype(jnp.bfloat16)
        y = y + jnp.dot(h, w2_ref[0, sl, :], preferred_element_type=jnp.float32)
    o_ref[0] = _layer_norm(alpha * x + y, gam_ref[...], bet_ref[...])


def _ffn(x, w1, w2, layer, gam, bet, alpha, t=512):
    b, s, d = x.shape
    f = w1.shape[2]
    return pl.pallas_call(
        functools.partial(_ffn_kernel, alpha=alpha, fchunk=1024),
        grid=(b, s // t),
        in_specs=[pl.BlockSpec((1, t, d), lambda bi, si: (bi, si, 0)),
                  pl.BlockSpec((1, d, f), lambda bi, si: (layer, 0, 0)),
                  pl.BlockSpec((1, f, d), lambda bi, si: (layer, 0, 0)),
                  pl.BlockSpec((1, d), lambda bi, si: (0, 0)),
                  pl.BlockSpec((1, d), lambda bi, si: (0, 0))],
        out_specs=pl.BlockSpec((1, t, d), lambda bi, si: (bi, si, 0)),
        out_shape=jax.ShapeDtypeStruct((b, s, d), jnp.float32),
        compiler_params=_params(("arbitrary", "arbitrary"), 56),
        name="ffn",
    )(x, w1, w2, gam, bet)


def _aug_weights(slopes):
    wa = np.zeros((N_BIAS_GROUPS, LANES, N_HEADS * LANES), np.float32)
    for h in range(N_HEADS):
        base = h * LANES + AUG_LANE
        sl = float(slopes[h])
        wa[G_DSA_Q, XE_A, base + 0] = -sl * CHUNK
        wa[G_DSA_Q, XE_B, base + 1] = -sl
        wa[G_DSA_Q, XE_ONE, base + 2] = 1.0
        wa[G_DSA_Q, XE_ONE, base + 3] = 1.0
        wa[G_DSA_K, XE_ONE, base + 0] = 1.0
        wa[G_DSA_K, XE_ONE, base + 1] = 1.0
        wa[G_DSA_K, XE_A, base + 2] = sl * CHUNK
        wa[G_DSA_K, XE_B, base + 3] = sl
        for p in range(3):
            wa[G_FOX_Q, XE_ONE, base + p] = 1.0
            wa[G_FOX_K, XE_C + N_HEADS * p + h, base + p] = -1.0
    return wa


def _in_offsets(d):
    sizes = (BRANCH, BRANCH, BRANCH, IDX_HEADS * IDX_DIM, IDX_DIM, IDX_HEADS,
             BRANCH, BRANCH, BRANCH, N_HEADS, BRANCH, BRANCH, BRANCH, N_BRANCHES * d)
    names = ("qa", "ka", "va", "qi", "ki", "wi", "qf", "kf", "vf", "fg", "qs", "ks", "vs", "gates")
    return dict(zip(names, np.concatenate([[0], np.cumsum(sizes)[:-1]]).tolist())), int(sum(sizes))


def _wlayout_kernel(w_ref, main_ref, small_ref, vt_ref, *, offs):
    t = w_ref.shape[2]
    scale = HEAD_DIM ** -0.5
    wide = N_HEADS * LANES

    def rows(sec, n):
        return w_ref[0, sec:sec + n, :]

    def padded_heads(sec, s):
        zeros = jnp.zeros((LANES - HEAD_DIM, t), jnp.float32)
        return jnp.concatenate([p for h in range(N_HEADS) for p in (rows(sec + h * HEAD_DIM, HEAD_DIM) * s, zeros)])

    def put(g, val):
        main_ref[0, g] = val.T.astype(jnp.bfloat16)

    put(G_DSA_Q, padded_heads(offs["qa"], scale))
    put(G_DSA_K, padded_heads(offs["ka"], 1.0))
    put(G_FOX_Q, padded_heads(offs["qf"], scale))
    put(G_FOX_K, padded_heads(offs["kf"], 1.0))
    put(G_SB, jnp.concatenate([rows(offs["qs"], BRANCH) * scale, rows(offs["ks"], BRANCH)]))
    put(G_QI, rows(offs["qi"], wide))
    for n in range(N_BRANCHES):
        put(G_GATES + n, rows(offs["gates"] + n * wide, wide))
    ki = rows(offs["ki"], IDX_DIM)
    small = jnp.concatenate([rows(offs["fg"], N_HEADS), rows(offs["wi"], IDX_HEADS),
                             jnp.zeros((LANES - N_HEADS - IDX_HEADS, t), jnp.float32), ki, ki])
    small_ref[0] = small.T.astype(jnp.bfloat16)
    vt_ref[0] = jnp.concatenate([rows(offs[name], BRANCH) for name in ("va", "vf", "vs")]).astype(jnp.bfloat16)


def _wlayout(w_in, t=256):
    depth, d, cols = w_in.shape
    offs, total = _in_offsets(d)
    assert total == cols and all(o % 8 == 0 for o in offs.values())
    wide = N_HEADS * LANES
    return pl.pallas_call(
        functools.partial(_wlayout_kernel, offs=offs),
        grid=(depth, d // t),
        in_specs=[pl.BlockSpec((1, cols, t), lambda l, r: (l, 0, r))],
        out_specs=[pl.BlockSpec((1, N_GROUPS, t, wide), lambda l, r: (l, 0, r, 0)),
                   pl.BlockSpec((1, t, 2 * LANES), lambda l, r: (l, r, 0)),
                   pl.BlockSpec((1, N_BRANCHES * BRANCH, t), lambda l, r: (l, 0, r))],
        out_shape=[jax.ShapeDtypeStruct((depth, N_GROUPS, d, wide), jnp.bfloat16),
                   jax.ShapeDtypeStruct((depth, d, 2 * LANES), jnp.bfloat16),
                   jax.ShapeDtypeStruct((depth, N_BRANCHES * BRANCH, d), jnp.bfloat16)],
        compiler_params=_params(("arbitrary", "arbitrary"), 56),
        name="wlayout",
    )(jnp.swapaxes(w_in, 1, 2))


def kernel(x, w_in, b_forget, w_branch, w_out, ln1_g, ln1_b, w_ff1, w_ff2, ln2_g, ln2_b):
    depth = w_in.shape[0]
    b, s, d = x.shape
    assert s % CHUNK == 0 and d == N_HEADS * LANES
    alpha = (2.0 * depth) ** 0.25
    topk = min(TOPK_MAX, s // 4)
    slopes = 2.0 ** (-8.0 * np.arange(1, N_HEADS + 1, dtype=np.float64) / N_HEADS)
    w_aug_np = _aug_weights(slopes)
    assert np.array_equal(w_aug_np, w_aug_np.astype(jnp.bfloat16).astype(np.float32))
    w_aug = jnp.asarray(w_aug_np, jnp.bfloat16)
    nc = s // CHUNK

    w_main, w_small, w_vt = _wlayout(w_in)
    wb, wo, w1, w2 = (w.astype(jnp.bfloat16) for w in (w_branch, w_out, w_ff1, w_ff2))
    for layer in range(depth):
        b_small = jnp.pad(b_forget[layer], (0, LANES - N_HEADS)).reshape(1, LANES)
        xb, xe, small, ki = _prep(x, w_small, layer, b_small)
        pq5 = _proj(xb, xe, w_main, layer, w_aug)
        pq = pq5.reshape(N_GROUPS, b, N_HEADS, nc, CHUNK, LANES)
        vt = _proj_vt(xb, w_vt, layer)
        wt = small[:, :, N_HEADS:N_HEADS + IDX_HEADS].transpose(0, 2, 1)
        o_a = _dsa(pq, vt, ki.reshape(b, nc, CHUNK, LANES), wt, topk)
        o_b = _causal_attention(_fox_kernel, "fox", _softmax_scratch, pq, vt,
                                (G_FOX_Q, 0, N_HEADS), (G_FOX_K, 0, N_HEADS), 1)
        o_c = _causal_attention(_sb_kernel, "sb", _sb_scratch, pq, vt,
                                (G_SB, 0, N_HEADS // 2), (G_SB, 1, N_HEADS // 2), 2)
        x = _merge(o_a, o_b, o_c, pq5, x, wb, wo, layer,
                   ln1_g[layer].reshape(1, d), ln1_b[layer].reshape(1, d), alpha)
        x = _ffn(x, w1, w2, layer, ln2_g[layer].reshape(1, d), ln2_b[layer].reshape(1, d), alpha)
    return x
```

```python
import functools

import numpy as np
import jax
import jax.numpy as jnp
from jax import lax
from jax.experimental import pallas as pl
from jax.experimental.pallas import tpu as pltpu

HEAD_DIM = 64
N_HEADS = 8
BRANCH = N_HEADS * HEAD_DIM
IDX_HEADS = 16
IDX_DIM = 64
TOPK_MAX = 256
N_BRANCHES = 3
LN_EPS = 1e-5

LANES = 128
CHUNK = 256
N_GROUPS = 9
G_DSA_Q, G_DSA_K, G_FOX_Q, G_FOX_K, G_SB, G_QI, G_GATES = 0, 1, 2, 3, 4, 5, 6
N_BIAS_GROUPS = 4
PROJ_BLOCK_LANES = 2
AUG_LANE = HEAD_DIM
XE_A, XE_B, XE_ONE, XE_C = 0, 1, 2, 3
NEG_BIG = -1e30
EXP_IS_ZERO_BELOW = -105.0
INT_MIN = -(2 ** 31)
KEY_OF_LOWEST_FINITE = INT_MIN + 2 ** 23

_NT = (((1,), (1,)), ((), ()))


def _log_sigmoid(x):
    return jnp.minimum(x, 0.0) - jnp.log(1.0 + jnp.exp(-jnp.abs(x)))


def _split3(x):
    a = x.astype(jnp.bfloat16)
    r = x - a.astype(jnp.float32)
    b = r.astype(jnp.bfloat16)
    c = (r - b.astype(jnp.float32)).astype(jnp.bfloat16)
    return a, b, c


def _layer_norm(z, g, b):
    mu = jnp.mean(z, axis=-1, keepdims=True)
    zc = z - mu
    var = jnp.mean(zc * zc, axis=-1, keepdims=True)
    return zc * lax.rsqrt(var + LN_EPS) * g + b


def _params(sem, vmem_mb):
    return pltpu.CompilerParams(dimension_semantics=sem, vmem_limit_bytes=vmem_mb * 1024 * 1024)


def _prep_kernel(x_ref, ws_ref, bs_ref, xb_ref, xe_ref, sm_ref, ki_ref, carry_ref):
    s_blk = pl.program_id(1)
    t = x_ref.shape[1]
    xb = x_ref[0].astype(jnp.bfloat16)
    xb_ref[0] = xb
    both = jnp.dot(xb, ws_ref[0], preferred_element_type=jnp.float32)
    sm = both[:, :LANES]
    sm_ref[0] = sm
    ki_ref[0] = both[:, LANES:].astype(jnp.bfloat16)
    fg = sm + bs_ref[...]
    lf = jnp.minimum(fg, 0.0) - jnp.log1p(jnp.exp(-jnp.abs(fg)))

    @pl.when(s_blk == 0)
    def _():
        carry_ref[...] = jnp.zeros_like(carry_ref)

    row = lax.broadcasted_iota(jnp.int32, (t, t), 0)
    col = lax.broadcasted_iota(jnp.int32, (t, t), 1)
    tri = jnp.where(col <= row, 1.0, 0.0).astype(jnp.bfloat16)
    cs = carry_ref[0:1, :]
    for part in _split3(lf):
        cs = cs + jnp.dot(tri, part, preferred_element_type=jnp.float32)
    carry_ref[...] = jnp.broadcast_to(cs[t - 1:t, :], carry_ref.shape)

    er = lax.broadcasted_iota(jnp.int32, (LANES, LANES), 0)
    ec = lax.broadcasted_iota(jnp.int32, (LANES, LANES), 1)
    xe = jnp.zeros((t, LANES), jnp.float32)
    for p, part in enumerate(_split3(cs)):
        place = jnp.where(ec == er + (XE_C + N_HEADS * p), jnp.where(er < N_HEADS, 1.0, 0.0), 0.0)
        xe = xe + jnp.dot(part, place.astype(jnp.bfloat16), preferred_element_type=jnp.float32)
    pos = s_blk * t + lax.broadcasted_iota(jnp.int32, (t, LANES), 0)
    lane = lax.broadcasted_iota(jnp.int32, (t, LANES), 1)
    xe = jnp.where(lane == XE_A, jnp.right_shift(pos, 8).astype(jnp.float32), xe)
    xe = jnp.where(lane == XE_B, (pos & (CHUNK - 1)).astype(jnp.float32), xe)
    xe = jnp.where(lane == XE_ONE, 1.0, xe)
    xe_ref[0] = xe.astype(jnp.bfloat16)


def _prep(x, w_small, layer, b_small, t=512):
    b, s, d = x.shape
    return pl.pallas_call(
        _prep_kernel,
        grid=(b, s // t),
        in_specs=[pl.BlockSpec((1, t, d), lambda bi, si: (bi, si, 0)),
                  pl.BlockSpec((1, d, 2 * LANES), lambda bi, si: (layer, 0, 0)),
                  pl.BlockSpec((1, LANES), lambda bi, si: (0, 0))],
        out_specs=[pl.BlockSpec((1, t, d), lambda bi, si: (bi, si, 0)),
                   pl.BlockSpec((1, t, LANES), lambda bi, si: (bi, si, 0)),
                   pl.BlockSpec((1, t, LANES), lambda bi, si: (bi, si, 0)),
                   pl.BlockSpec((1, t, LANES), lambda bi, si: (bi, si, 0))],
        out_shape=[jax.ShapeDtypeStruct((b, s, d), jnp.bfloat16),
                   jax.ShapeDtypeStruct((b, s, LANES), jnp.bfloat16),
                   jax.ShapeDtypeStruct((b, s, LANES), jnp.float32),
                   jax.ShapeDtypeStruct((b, s, LANES), jnp.bfloat16)],
        scratch_shapes=[pltpu.VMEM((8, LANES), jnp.float32)],
        compiler_params=_params(("arbitrary", "arbitrary"), 32),
        name="prep",
    )(x, w_small, b_small)


def _proj_kernel(xb_ref, xe_ref, w_ref, wa_ref, o_ref):
    j = pl.program_id(0)

    def column_blocks(with_bias):
        is_gate = jnp.logical_and(j >= G_GATES, j < G_GATES + N_BRANCHES)
        width = PROJ_BLOCK_LANES * LANES
        for c in range(N_HEADS // PROJ_BLOCK_LANES):
            sl = slice(c * width, (c + 1) * width)
            acc = jnp.dot(xb_ref[0], w_ref[0, 0, :, sl], preferred_element_type=jnp.float32)
            if with_bias:
                acc = acc + jnp.dot(xe_ref[0], wa_ref[0, :, sl], preferred_element_type=jnp.float32)
            else:
                acc = jnp.where(is_gate, jax.nn.sigmoid(acc), acc)
            for g in range(PROJ_BLOCK_LANES):
                o_ref[0, 0, PROJ_BLOCK_LANES * c + g] = acc[:, g * LANES:(g + 1) * LANES].astype(jnp.bfloat16)

    @pl.when(j < N_BIAS_GROUPS)
    def _():
        column_blocks(True)

    @pl.when(j >= N_BIAS_GROUPS)
    def _():
        column_blocks(False)


def _proj(xb, xe, w_main, layer, w_aug, t=1024):
    b, s, d = xb.shape
    assert s % t == 0
    ns = s // t
    return pl.pallas_call(
        _proj_kernel,
        grid=(N_GROUPS, b * ns),
        in_specs=[pl.BlockSpec((1, t, d), lambda j, i: (i // ns, i % ns, 0)),
                  pl.BlockSpec((1, t, LANES), lambda j, i: (i // ns, i % ns, 0)),
                  pl.BlockSpec((1, 1, d, N_HEADS * LANES), lambda j, i: (layer, j, 0, 0)),
                  pl.BlockSpec((1, LANES, N_HEADS * LANES), lambda j, i: (jnp.minimum(j, N_BIAS_GROUPS - 1), 0, 0))],
        out_specs=pl.BlockSpec((1, 1, N_HEADS, t, LANES), lambda j, i: (j, i // ns, 0, i % ns, 0)),
        out_shape=jax.ShapeDtypeStruct((N_GROUPS, b, N_HEADS, s, LANES), jnp.bfloat16),
        compiler_params=_params(("arbitrary", "arbitrary"), 48),
        name="proj",
    )(xb, xe, w_main, w_aug)


def _proj_vt_kernel(xb_ref, w_ref, o_ref):
    vt = lax.dot_general(w_ref[0], xb_ref[0], _NT, preferred_element_type=jnp.float32)
    for c in range(xb_ref.shape[1] // CHUNK):
        blk = vt[:, c * CHUNK:(c + 1) * CHUNK].astype(jnp.bfloat16)
        o_ref[0, :, c] = blk.reshape(N_BRANCHES * N_HEADS, HEAD_DIM, CHUNK)


def _proj_vt(xb, w_vt, layer, t=1024):
    b, s, d = xb.shape
    nh = N_BRANCHES * N_HEADS
    return pl.pallas_call(
        _proj_vt_kernel,
        grid=(b, s // t),
        in_specs=[pl.BlockSpec((1, t, d), lambda bi, si: (bi, si, 0)),
                  pl.BlockSpec((1, nh * HEAD_DIM, d), lambda bi, si: (layer, 0, 0))],
        out_specs=pl.BlockSpec((1, nh, t // CHUNK, HEAD_DIM, CHUNK), lambda bi, si: (bi, 0, si, 0, 0)),
        out_shape=jax.ShapeDtypeStruct((b, nh, s // CHUNK, HEAD_DIM, CHUNK), jnp.bfloat16),
        compiler_params=_params(("arbitrary", "arbitrary"), 48),
        name="proj_vt",
    )(xb, w_vt)


STAGE_LAG = 8
COUNT_ACCS = 4


def _staged(stages):
    for step in range(N_HEADS + STAGE_LAG * (len(stages) - 1)):
        for n, stage in enumerate(stages):
            if 0 <= step - STAGE_LAG * n < N_HEADS:
                stage(step - STAGE_LAG * n)


def _softmax_tile(j0, n, q_ref, k_ref, vt_ref, mask, refs):
    s_ref, mloc_ref, m_ref, l_ref, acc_ref = refs
    rows = n * CHUNK

    def scores(h):
        keys = k_ref[0, 0, h, pl.ds(j0, n)].reshape(rows, LANES)
        s = mask(lax.dot_general(keys, q_ref[0, 0, h, 0], _NT, preferred_element_type=jnp.float32))
        s_ref[h, :rows] = s
        mloc_ref[h] = jnp.max(s, axis=0, keepdims=True)

    def update(h):
        m = m_ref[h]
        m_new = jnp.maximum(m, mloc_ref[h])
        alpha = jnp.exp(m - m_new)
        p = jnp.exp(s_ref[h, :rows] - m_new)
        m_ref[h] = m_new
        l_ref[h] = alpha * l_ref[h] + jnp.sum(p, axis=0, keepdims=True)
        pb = p.astype(jnp.bfloat16)
        pv = jnp.dot(vt_ref[0, h, j0], pb[:CHUNK], preferred_element_type=jnp.float32)
        for c in range(1, n):
            pv = pv + jnp.dot(vt_ref[0, h, j0 + c], pb[c * CHUNK:(c + 1) * CHUNK], preferred_element_type=jnp.float32)
        acc_ref[h] = alpha * acc_ref[h] + pv

    _staged([scores, update])


def _split_head_pairs(q_ref, qz_ref, n_heads):
    tq = q_ref.shape[4]
    lane = lax.broadcasted_iota(jnp.int32, (tq, LANES), 1)
    for p in range(n_heads // 2):
        qp = q_ref[0, 0, p, 0]
        qz_ref[2 * p] = jnp.where(lane < HEAD_DIM, qp, jnp.zeros_like(qp))
        qz_ref[2 * p + 1] = jnp.where(lane >= HEAD_DIM, qp, jnp.zeros_like(qp))


def _store_heads(o_ref, head_out):
    for p in range(N_HEADS // 2):
        pair = jnp.concatenate([head_out(2 * p), head_out(2 * p + 1)], axis=0)
        o_ref[0, :, p * LANES:(p + 1) * LANES] = pair.T.astype(o_ref.dtype)


def _softmax_init(m_ref, l_ref, acc_ref):
    m_ref[...] = jnp.full(m_ref.shape, NEG_BIG, jnp.float32)
    l_ref[...] = jnp.zeros(l_ref.shape, jnp.float32)
    acc_ref[...] = jnp.zeros(acc_ref.shape, jnp.float32)


TILE_CHUNKS = 4


def _softmax_scratch(tq):
    vec = pltpu.VMEM((N_HEADS, 1, tq), jnp.float32)
    return [pltpu.VMEM((N_HEADS, TILE_CHUNKS * CHUNK, tq), jnp.float32), vec, vec, vec,
            pltpu.VMEM((N_HEADS, HEAD_DIM, tq), jnp.float32)]


def _softmax_chunks(n_chunks, tile):
    n_tiles = n_chunks // TILE_CHUNKS

    def body(t, _):
        tile(t * TILE_CHUNKS, TILE_CHUNKS)
        return 0

    lax.fori_loop(0, n_tiles, body, 0)
    done = n_tiles * TILE_CHUNKS
    size = TILE_CHUNKS // 2
    while size >= 1:
        has = ((n_chunks - done) & size) != 0

        @pl.when(has)
        def _(done=done, size=size):
            tile(done, size)

        done = done + jnp.where(has, size, 0)
        size //= 2


def _dsa_kernel(q_ref, k_ref, vt_ref, qi_ref, ki_ref, wt_ref, o_ref, sc_ref, scb_ref, qz_ref,
                s_ref, mloc_ref, m_ref, l_ref, acc_ref, *, topk):
    i = pl.program_id(1)
    tq = q_ref.shape[4]
    _split_head_pairs(qi_ref, qz_ref, IDX_HEADS)
    rel = (lax.broadcasted_iota(jnp.int32, (CHUNK, tq), 0)
           - lax.broadcasted_iota(jnp.int32, (CHUNK, tq), 1))

    def score_chunks(j0, n):
        kc = ki_ref[0, pl.ds(j0, n)].reshape(n * CHUNK, LANES)
        sc = jnp.zeros((n * CHUNK, tq), jnp.float32)
        for h in range(IDX_HEADS):
            d = lax.dot_general(kc, qz_ref[h], _NT, preferred_element_type=jnp.float32)
            sc = sc + jnp.maximum(d, 0.0) * wt_ref[0, h:h + 1, :]
        for c in range(n):
            masked = jnp.where(rel <= (i - j0 - c) * CHUNK, sc[c * CHUNK:(c + 1) * CHUNK], -jnp.inf)
            sc_ref[j0 + c] = masked
            scb_ref[j0 + c] = masked.astype(jnp.bfloat16)

    def score_pair(t, _):
        score_chunks(2 * t, 2)
        return 0

    lax.fori_loop(0, (i + 1) // 2, score_pair, 0)

    @pl.when((i + 1) % 2 == 1)
    def _():
        score_chunks(i, 1)

    def ordered_float(key):
        return pltpu.bitcast(key ^ ((key >> 31) & jnp.int32(0x7FFFFFFF)), jnp.float32)

    def count_ge(cand):
        def body(j, acc):
            ind = jnp.where(sc_ref[j] >= cand, 1.0, 0.0)
            return acc + jnp.sum(ind.reshape(CHUNK // (8 * COUNT_ACCS), COUNT_ACCS * 8, tq), axis=0)
        acc = lax.fori_loop(0, i + 1, body, jnp.zeros((COUNT_ACCS * 8, tq), jnp.float32))
        return jnp.sum(acc, axis=0, keepdims=True)

    rows16 = 16 * COUNT_ACCS

    def count_ge16(k16):
        f = pltpu.bitcast((k16 ^ ((k16 >> 15) & jnp.int32(0x7FFF))) << 16, jnp.float32)
        cand = jnp.broadcast_to(f, (rows16, tq)).astype(jnp.bfloat16)

        def body(j, acc):
            for g in range(CHUNK // rows16):
                ge = scb_ref[j, g * rows16:(g + 1) * rows16, :] >= cand
                acc = acc + jnp.where(ge, jnp.bfloat16(1), jnp.bfloat16(0))
            return acc

        acc = lax.fori_loop(0, i + 1, body, jnp.zeros((rows16, tq), jnp.bfloat16))
        return jnp.sum(acc.astype(jnp.float32), axis=0, keepdims=True)

    k16 = jnp.where(count_ge16(jnp.zeros((1, tq), jnp.int32)) >= topk, jnp.int32(0), jnp.int32(-(2 ** 15)))

    def bit_body16(n, k16):
        cand = k16 + jnp.left_shift(jnp.int32(1), 14 - n)
        return jnp.where(count_ge16(cand) >= topk, cand, k16)

    k16 = lax.fori_loop(0, 15, bit_body16, k16)
    at_t = (k16 << 16) | jnp.where(k16 < 0, jnp.int32(0xFFFF), jnp.int32(0))
    key = jnp.maximum(at_t, jnp.int32(INT_MIN + 2 ** 15 + 1)) - jnp.int32(2 ** 15 + 1)

    def bit_body(n, key):
        cand = key + jnp.left_shift(jnp.int32(1), 16 - n)
        return jnp.where(count_ge(ordered_float(cand)) >= topk, cand, key)

    key = lax.fori_loop(0, 17, bit_body, key)
    thr = ordered_float(jnp.maximum(key, jnp.int32(KEY_OF_LOWEST_FINITE)))

    _softmax_init(m_ref, l_ref, acc_ref)

    def tile(j0, n):
        selected = sc_ref[pl.ds(j0, n)].reshape(n * CHUNK, tq) >= thr
        _softmax_tile(j0, n, q_ref, k_ref, vt_ref, lambda s: jnp.where(selected, s, NEG_BIG),
                      (s_ref, mloc_ref, m_ref, l_ref, acc_ref))

    _softmax_chunks(i + 1, tile)
    _store_heads(o_ref, lambda h: acc_ref[h] / l_ref[h])


def _dsa(pq, vt, ki, wt, topk):
    _, b, _, nc, _, _ = pq.shape
    tq = CHUNK
    kern = functools.partial(_dsa_kernel, topk=float(topk))
    return pl.pallas_call(
        kern,
        grid=(b, nc),
        in_specs=[pl.BlockSpec((1, 1, N_HEADS, 1, tq, LANES), lambda bi, i: (G_DSA_Q, bi, 0, i, 0, 0)),
                  pl.BlockSpec((1, 1, N_HEADS, nc, CHUNK, LANES), lambda bi, i: (G_DSA_K, bi, 0, 0, 0, 0)),
                  pl.BlockSpec((1, N_HEADS, nc, HEAD_DIM, CHUNK), lambda bi, i: (bi, 0, 0, 0, 0)),
                  pl.BlockSpec((1, 1, N_HEADS, 1, tq, LANES), lambda bi, i: (G_QI, bi, 0, i, 0, 0)),
                  pl.BlockSpec((1, nc, CHUNK, LANES), lambda bi, i: (bi, 0, 0, 0)),
                  pl.BlockSpec((1, IDX_HEADS, tq), lambda bi, i: (bi, 0, i))],
        out_specs=pl.BlockSpec((1, tq, BRANCH), lambda bi, i: (bi, i, 0)),
        out_shape=jax.ShapeDtypeStruct((b, nc * CHUNK, BRANCH), jnp.bfloat16),
        scratch_shapes=[pltpu.VMEM((nc, CHUNK, tq), jnp.float32), pltpu.VMEM((nc, CHUNK, tq), jnp.bfloat16),
                        pltpu.VMEM((IDX_HEADS, tq, LANES), jnp.bfloat16)] + _softmax_scratch(tq),
        compiler_params=_params(("arbitrary", "arbitrary"), 56),
        name="dsa",
    )(pq, pq, vt, pq, ki, wt)


def _fox_kernel(q_ref, k_ref, vt_ref, o_ref, s_ref, mloc_ref, m_ref, l_ref, acc_ref):
    i = pl.program_id(1)
    tq = q_ref.shape[4]
    causal = (lax.broadcasted_iota(jnp.int32, (CHUNK, tq), 0)
              <= lax.broadcasted_iota(jnp.int32, (CHUNK, tq), 1))
    _softmax_init(m_ref, l_ref, acc_ref)
    refs = (s_ref, mloc_ref, m_ref, l_ref, acc_ref)
    _softmax_chunks(i, lambda j0, n: _softmax_tile(j0, n, q_ref, k_ref, vt_ref, lambda s: s, refs))
    _softmax_tile(i, 1, q_ref, k_ref, vt_ref, lambda s: jnp.where(causal, s, NEG_BIG), refs)
    _store_heads(o_ref, lambda h: acc_ref[h] / l_ref[h])


def _sb_scratch(tq):
    tile = pltpu.VMEM((N_HEADS, CHUNK, tq), jnp.bfloat16)
    vec = pltpu.VMEM((N_HEADS, 1, tq), jnp.float32)
    return [pltpu.VMEM((N_HEADS, tq, LANES), jnp.bfloat16),
            pltpu.VMEM((N_HEADS, CHUNK, tq), jnp.float32), tile, tile, tile, vec, vec,
            pltpu.VMEM((N_HEADS, HEAD_DIM, tq), jnp.float32)]


def _sb_kernel(q_ref, k_ref, vt_ref, o_ref, qz_ref, lb_ref, hi_ref, lo_ref, a_ref, later_ref, lom0_ref, acc_ref):
    i = pl.program_id(1)
    tq = q_ref.shape[4]
    strict = (lax.broadcasted_iota(jnp.int32, (CHUNK, tq), 0)
              < lax.broadcasted_iota(jnp.int32, (CHUNK, tq), 1))
    upper = jnp.where(lax.broadcasted_iota(jnp.int32, (CHUNK, CHUNK), 1)
                      > lax.broadcasted_iota(jnp.int32, (CHUNK, CHUNK), 0), 1.0, 0.0).astype(jnp.bfloat16)
    later_ref[...] = jnp.zeros(later_ref.shape, jnp.float32)
    acc_ref[...] = jnp.zeros(acc_ref.shape, jnp.float32)
    _split_head_pairs(q_ref, qz_ref, N_HEADS)

    def chunk(j, diag):
        def logits(h):
            z = lax.dot_general(k_ref[0, 0, h // 2, j], qz_ref[h], _NT, preferred_element_type=jnp.float32)
            log_beta = _log_sigmoid(z)
            lom = log_beta - z
            if diag:
                lom = jnp.where(strict, lom, 0.0)
            hi = pltpu.bitcast(pltpu.bitcast(lom, jnp.int32) & jnp.int32(-65536), jnp.float32)
            lb_ref[h] = log_beta
            hi_ref[h] = hi.astype(jnp.bfloat16)
            lo_ref[h] = (lom - hi).astype(jnp.bfloat16)
            lom0_ref[h] = lom[0:1, :]

        def weights(h):
            within = (jnp.dot(upper, hi_ref[h], preferred_element_type=jnp.float32)
                      + jnp.dot(upper, lo_ref[h], preferred_element_type=jnp.float32))
            a = jnp.exp(lb_ref[h] + within + later_ref[h])
            if diag:
                a = jnp.where(strict, a, 0.0)
            a_ref[h] = a.astype(jnp.bfloat16)
            later_ref[h] += within[0:1, :] + lom0_ref[h]

        def values(h):
            acc_ref[h] += jnp.dot(vt_ref[0, h, j], a_ref[h], preferred_element_type=jnp.float32)

        _staged([logits, weights, values])
        return 0

    chunk(i, True)

    def live(state):
        n, later_max = state
        return jnp.logical_and(n < i, later_max > EXP_IS_ZERO_BELOW)

    def step(state):
        n, _ = state
        chunk(i - 1 - n, False)
        return n + 1, jnp.max(later_ref[...])

    lax.while_loop(live, step, (jnp.int32(0), jnp.max(later_ref[...])))
    _store_heads(o_ref, lambda h: acc_ref[h])


def _causal_attention(kernel, name, scratch, pq, vt, q_at, k_at, branch):
    _, b, _, nc, _, _ = pq.shape
    tq = CHUNK
    (g_q, q_blk, q_n), (g_k, k_blk, k_n) = q_at, k_at
    return pl.pallas_call(
        kernel,
        grid=(b, nc),
        in_specs=[pl.BlockSpec((1, 1, q_n, 1, tq, LANES), lambda bi, i: (g_q, bi, q_blk, i, 0, 0)),
                  pl.BlockSpec((1, 1, k_n, nc, CHUNK, LANES), lambda bi, i: (g_k, bi, k_blk, 0, 0, 0)),
                  pl.BlockSpec((1, N_HEADS, nc, HEAD_DIM, CHUNK), lambda bi, i: (bi, branch, 0, 0, 0))],
        out_specs=pl.BlockSpec((1, tq, BRANCH), lambda bi, i: (bi, i, 0)),
        out_shape=jax.ShapeDtypeStruct((b, nc * CHUNK, BRANCH), jnp.bfloat16),
        scratch_shapes=scratch(tq),
        compiler_params=_params(("arbitrary", "arbitrary"), 48),
        name=name,
    )(pq, pq, vt)


def _merge_kernel(oa_ref, ob_ref, oc_ref, g_ref, x_ref, wb_ref, wo_ref, gam_ref, bet_ref, o_ref, *, alpha):
    ups = [jnp.dot(o[0], wb_ref[0, n], preferred_element_type=jnp.float32)
           for n, o in enumerate((oa_ref, ob_ref, oc_ref))]
    cols = []
    for c in range(N_HEADS):
        sl = slice(c * LANES, (c + 1) * LANES)
        mc = ups[0][:, sl] * g_ref[0, 0, c].astype(jnp.float32)
        for n in range(1, N_BRANCHES):
            mc = mc + ups[n][:, sl] * g_ref[n, 0, c].astype(jnp.float32)
        cols.append(mc.astype(jnp.bfloat16))
    merged = jnp.concatenate(cols, axis=1)
    y = jnp.dot(merged, wo_ref[0], preferred_element_type=jnp.float32)
    o_ref[0] = _layer_norm(alpha * x_ref[0] + y, gam_ref[...], bet_ref[...])


def _merge(oa, ob, oc, pq5, x, wb, wo, layer, gam, bet, alpha, t=512):
    b, s, d = x.shape
    o_spec = pl.BlockSpec((1, t, BRANCH), lambda bi, si: (bi, si, 0))
    return pl.pallas_call(
        functools.partial(_merge_kernel, alpha=alpha),
        grid=(b, s // t),
        in_specs=[o_spec, o_spec, o_spec,
                  pl.BlockSpec((N_BRANCHES, 1, N_HEADS, t, LANES), lambda bi, si: (G_GATES // N_BRANCHES, bi, 0, si, 0)),
                  pl.BlockSpec((1, t, d), lambda bi, si: (bi, si, 0)),
                  pl.BlockSpec((1, N_BRANCHES, BRANCH, d), lambda bi, si: (layer, 0, 0, 0)),
                  pl.BlockSpec((1, d, d), lambda bi, si: (layer, 0, 0)),
                  pl.BlockSpec((1, d), lambda bi, si: (0, 0)),
                  pl.BlockSpec((1, d), lambda bi, si: (0, 0))],
        out_specs=pl.BlockSpec((1, t, d), lambda bi, si: (bi, si, 0)),
        out_shape=jax.ShapeDtypeStruct((b, s, d), jnp.float32),
        compiler_params=_params(("arbitrary", "arbitrary"), 48),
        name="merge",
    )(oa, ob, oc, pq5, x, wb, wo, gam, bet)


def _ffn_kernel(x_ref, w1_ref, w2_ref, gam_ref, bet_ref, o_ref, *, alpha, fchunk):
    x = x_ref[0]
    xb = x.astype(jnp.bfloat16)
    y = jnp.zeros_like(x)
    for c in range(w1_ref.shape[2] // fchunk):
        sl = slice(c * fchunk, (c + 1) * fchunk)
        h = jnp.dot(xb, w1_ref[0, :, sl], preferred_element_type=jnp.float32)
        h = jnp.square(jnp.maximum(h, 0.0)).astype(jnp.bfloat16)
        y = y + jnp.dot(h, w2_ref[0, sl, :], preferred_element_type=jnp.float32)
    o_ref[0] = _layer_norm(alpha * x + y, gam_ref[...], bet_ref[...])


def _ffn(x, w1, w2, layer, gam, bet, alpha, t=512):
    b, s, d = x.shape
    f = w1.shape[2]
    return pl.pallas_call(
        functools.partial(_ffn_kernel, alpha=alpha, fchunk=1024),
        grid=(b, s // t),
        in_specs=[pl.BlockSpec((1, t, d), lambda bi, si: (bi, si, 0)),
                  pl.BlockSpec((1, d, f), lambda bi, si: (layer, 0, 0)),
                  pl.BlockSpec((1, f, d), lambda bi, si: (layer, 0, 0)),
                  pl.BlockSpec((1, d), lambda bi, si: (0, 0)),
                  pl.BlockSpec((1, d), lambda bi, si: (0, 0))],
        out_specs=pl.BlockSpec((1, t, d), lambda bi, si: (bi, si, 0)),
        out_shape=jax.ShapeDtypeStruct((b, s, d), jnp.float32),
        compiler_params=_params(("arbitrary", "arbitrary"), 56),
        name="ffn",
    )(x, w1, w2, gam, bet)


def _aug_weights(slopes):
    wa = np.zeros((N_BIAS_GROUPS, LANES, N_HEADS * LANES), np.float32)
    for h in range(N_HEADS):
        base = h * LANES + AUG_LANE
        sl = float(slopes[h])
        wa[G_DSA_Q, XE_A, base + 0] = -sl * CHUNK
        wa[G_DSA_Q, XE_B, base + 1] = -sl
        wa[G_DSA_Q, XE_ONE, base + 2] = 1.0
        wa[G_DSA_Q, XE_ONE, base + 3] = 1.0
        wa[G_DSA_K, XE_ONE, base + 0] = 1.0
        wa[G_DSA_K, XE_ONE, base + 1] = 1.0
        wa[G_DSA_K, XE_A, base + 2] = sl * CHUNK
        wa[G_DSA_K, XE_B, base + 3] = sl
        for p in range(3):
            wa[G_FOX_Q, XE_ONE, base + p] = 1.0
            wa[G_FOX_K, XE_C + N_HEADS * p + h, base + p] = -1.0
    return wa


def _in_offsets(d):
    sizes = (BRANCH, BRANCH, BRANCH, IDX_HEADS * IDX_DIM, IDX_DIM, IDX_HEADS,
             BRANCH, BRANCH, BRANCH, N_HEADS, BRANCH, BRANCH, BRANCH, N_BRANCHES * d)
    names = ("qa", "ka", "va", "qi", "ki", "wi", "qf", "kf", "vf", "fg", "qs", "ks", "vs", "gates")
    return dict(zip(names, np.concatenate([[0], np.cumsum(sizes)[:-1]]).tolist())), int(sum(sizes))


def _wlayout_kernel(w_ref, main_ref, small_ref, vt_ref, *, offs):
    t = w_ref.shape[2]
    scale = HEAD_DIM ** -0.5
    wide = N_HEADS * LANES

    def rows(sec, n):
        return w_ref[0, sec:sec + n, :]

    def padded_heads(sec, s):
        zeros = jnp.zeros((LANES - HEAD_DIM, t), jnp.float32)
        return jnp.concatenate([p for h in range(N_HEADS) for p in (rows(sec + h * HEAD_DIM, HEAD_DIM) * s, zeros)])

    def put(g, val):
        main_ref[0, g] = val.T.astype(jnp.bfloat16)

    put(G_DSA_Q, padded_heads(offs["qa"], scale))
    put(G_DSA_K, padded_heads(offs["ka"], 1.0))
    put(G_FOX_Q, padded_heads(offs["qf"], scale))
    put(G_FOX_K, padded_heads(offs["kf"], 1.0))
    put(G_SB, jnp.concatenate([rows(offs["qs"], BRANCH) * scale, rows(offs["ks"], BRANCH)]))
    put(G_QI, rows(offs["qi"], wide))
    for n in range(N_BRANCHES):
        put(G_GATES + n, rows(offs["gates"] + n * wide, wide))
    ki = rows(offs["ki"], IDX_DIM)
    small = jnp.concatenate([rows(offs["fg"], N_HEADS), rows(offs["wi"], IDX_HEADS),
                             jnp.zeros((LANES - N_HEADS - IDX_HEADS, t), jnp.float32), ki, ki])
    small_ref[0] = small.T.astype(jnp.bfloat16)
    vt_ref[0] = jnp.concatenate([rows(offs[name], BRANCH) for name in ("va", "vf", "vs")]).astype(jnp.bfloat16)


def _wlayout(w_in, t=256):
    depth, d, cols = w_in.shape
    offs, total = _in_offsets(d)
    assert total == cols and all(o % 8 == 0 for o in offs.values())
    wide = N_HEADS * LANES
    return pl.pallas_call(
        functools.partial(_wlayout_kernel, offs=offs),
        grid=(depth, d // t),
        in_specs=[pl.BlockSpec((1, cols, t), lambda l, r: (l, 0, r))],
        out_specs=[pl.BlockSpec((1, N_GROUPS, t, wide), lambda l, r: (l, 0, r, 0)),
                   pl.BlockSpec((1, t, 2 * LANES), lambda l, r: (l, r, 0)),
                   pl.BlockSpec((1, N_BRANCHES * BRANCH, t), lambda l, r: (l, 0, r))],
        out_shape=[jax.ShapeDtypeStruct((depth, N_GROUPS, d, wide), jnp.bfloat16),
                   jax.ShapeDtypeStruct((depth, d, 2 * LANES), jnp.bfloat16),
                   jax.ShapeDtypeStruct((depth, N_BRANCHES * BRANCH, d), jnp.bfloat16)],
        compiler_params=_params(("arbitrary", "arbitrary"), 56),
        name="wlayout",
    )(jnp.swapaxes(w_in, 1, 2))


def kernel(x, w_in, b_forget, w_branch, w_out, ln1_g, ln1_b, w_ff1, w_ff2, ln2_g, ln2_b):
    depth = w_in.shape[0]
    b, s, d = x.shape
    assert s % CHUNK == 0 and d == N_HEADS * LANES
    alpha = (2.0 * depth) ** 0.25
    topk = min(TOPK_MAX, s // 4)
    slopes = 2.0 ** (-8.0 * np.arange(1, N_HEADS + 1, dtype=np.float64) / N_HEADS)
    w_aug_np = _aug_weights(slopes)
    assert np.array_equal(w_aug_np, w_aug_np.astype(jnp.bfloat16).astype(np.float32))
    w_aug = jnp.asarray(w_aug_np, jnp.bfloat16)
    nc = s // CHUNK

    w_main, w_small, w_vt = _wlayout(w_in)
    wb, wo, w1, w2 = (w.astype(jnp.bfloat16) for w in (w_branch, w_out, w_ff1, w_ff2))
    for layer in range(depth):
        b_small = jnp.pad(b_forget[layer], (0, LANES - N_HEADS)).reshape(1, LANES)
        xb, xe, small, ki = _prep(x, w_small, layer, b_small)
        pq5 = _proj(xb, xe, w_main, layer, w_aug)
        pq = pq5.reshape(N_GROUPS, b, N_HEADS, nc, CHUNK, LANES)
        vt = _proj_vt(xb, w_vt, layer)
        wt = small[:, :, N_HEADS:N_HEADS + IDX_HEADS].transpose(0, 2, 1)
        o_a = _dsa(pq, vt, ki.reshape(b, nc, CHUNK, LANES), wt, topk)
        o_b = _causal_attention(_fox_kernel, "fox", _softmax_scratch, pq, vt,
                                (G_FOX_Q, 0, N_HEADS), (G_FOX_K, 0, N_HEADS), 1)
        o_c = _causal_attention(_sb_kernel, "sb", _sb_scratch, pq, vt,
                                (G_SB, 0, N_HEADS // 2), (G_SB, 1, N_HEADS // 2), 2)
        x = _merge(o_a, o_b, o_c, pq5, x, wb, wo, layer,
                   ln1_g[layer].reshape(1, d), ln1_b[layer].reshape(1, d), alpha)
        x = _ffn(x, w1, w2, layer, ln2_g[layer].reshape(1, d), ln2_b[layer].reshape(1, d), alpha)
    return x
```
